```python
import jax, jax.numpy as jnp
from jax import lax
import numpy as np

D_MODEL = 1024
BATCH = 8
SEQ = 4096
DEPTH = 2

CHUNK = 64
EXPAND = 2
D_INNER = EXPAND * D_MODEL
POOL_WINDOWS = (2, 4, 8, 16)
N_POOL_GROUPS = len(POOL_WINDOWS)
POOL_GROUP = D_INNER // N_POOL_GROUPS
HGRN_HEAD_DIM = 128
HGRN_HEADS = D_INNER // HGRN_HEAD_DIM
N_MIXERS = 2
N_POOL_LAYERS = (DEPTH + 1) // 2
N_HGRN_LAYERS = DEPTH // 2
EPS = 1e-6

kernel_name = "hybrid_pool_hgrn2_stream_encoder"


def rms_norm(x, g):
    xf = x.astype(jnp.float32)
    y = xf * lax.rsqrt(jnp.mean(xf * xf, axis=-1, keepdims=True) + EPS)
    return (y * g.astype(jnp.float32)).astype(x.dtype)


def multiscale_pool(v):
    T = v.shape[1]
    vf = v.astype(jnp.float32)
    cs = jnp.cumsum(vf, axis=1)
    pos = jnp.arange(T, dtype=jnp.float32)
    outs = []
    for g, w in enumerate(POOL_WINDOWS):
        lo, hi = g * POOL_GROUP, (g + 1) * POOL_GROUP
        c = cs[..., lo:hi]
        prev = jnp.pad(c[:, :-w], ((0, 0), (w, 0), (0, 0)))
        cnt = jnp.minimum(pos + 1.0, float(w))[None, :, None]
        outs.append((c - prev) / cnt - vf[..., lo:hi])
    return jnp.stack(outs, axis=2).astype(v.dtype)


def pool_mixer(h, w_in, w_grp, scale, w_out):
    B, T, _ = h.shape
    u = h @ w_in
    v, gate = jnp.split(u, 2, axis=-1)
    p = multiscale_pool(v)
    m = jnp.einsum('btgc,gcd->btgd', p, w_grp).reshape(B, T, D_INNER) * scale
    return (m * jax.nn.silu(gate)) @ w_out


def hgrn2_chunk_scan(q, k, v, b):
    _, Bn, H, C, dk = q.shape
    dv = v.shape[-1]
    causal = jnp.tril(jnp.ones((C, C), dtype=bool))

    def step(S, xs):
        qc, kc, vc, bc = xs
        diff = bc[:, :, :, None, :] - bc[:, :, None, :, :]
        diff = jnp.where(causal[None, None, :, :, None], diff, -jnp.inf)
        A = jnp.einsum('bhtd,bhsd,bhtsd->bhts', qc, kc, jnp.exp(diff))
        o = (jnp.einsum('bhts,bhsv->bhtv', A, vc)
             + jnp.einsum('bhtd,bhdv->bhtv', qc * jnp.exp(bc), S))
        b_last = bc[:, :, -1:, :]
        S = (jnp.exp(b_last[:, :, 0, :])[..., None] * S
             + jnp.einsum('bhsd,bhsv->bhdv', kc * jnp.exp(b_last - bc), vc))
        return S, o

    S0 = jnp.zeros((Bn, H, dk, dv), jnp.float32)
    _, o = lax.scan(step, S0, (q, k, v, b))
    return o


def hgrn2_mixer(h, w_in, lb, norm_g, w_out):
    B, T, _ = h.shape
    N = T // CHUNK
    u = h @ w_in
    q, fz, i, gate = jnp.split(u, 4, axis=-1)
    fz = fz.astype(jnp.float32)
    lbf = lb.astype(jnp.float32)
    log_f = jnp.logaddexp(jnp.log(lbf), jnp.log1p(-lbf) + jax.nn.log_sigmoid(fz))
    k = (1.0 - lbf) * jax.nn.sigmoid(-fz)

    def to_chunks(a):
        return a.astype(jnp.float32).reshape(B, N, CHUNK, HGRN_HEADS, HGRN_HEAD_DIM).transpose(1, 0, 3, 2, 4)

    qc, kc, vc, lfc = to_chunks(q), to_chunks(k), to_chunks(i), to_chunks(log_f)
    bc = jnp.cumsum(lfc, axis=3)
    o = hgrn2_chunk_scan(qc, kc, vc, bc)
    o = o.transpose(1, 0, 3, 2, 4).reshape(B, T, HGRN_HEADS, HGRN_HEAD_DIM)
    o = o * lax.rsqrt(jnp.mean(o * o, axis=-1, keepdims=True) + EPS)
    o = (o * norm_g.astype(jnp.float32).reshape(HGRN_HEADS, HGRN_HEAD_DIM)).reshape(B, T, D_INNER)
    o = o.astype(h.dtype)
    return (o * jax.nn.silu(gate)) @ w_out


def setup_inputs(seed: int = 0) -> dict:
    key = jax.random.key(seed)
    ks = jax.random.split(key, 12)
    f32 = jnp.float32
    x = jax.random.normal(ks[0], (BATCH, SEQ, D_MODEL), f32)
    norm_g = 1.0 + 0.05 * jax.random.normal(ks[1], (DEPTH, D_MODEL), f32)
    pool_w_in = jax.random.normal(ks[2], (N_POOL_LAYERS, D_MODEL, 2 * D_INNER), f32) * D_MODEL ** -0.5
    pool_w_grp = jax.random.normal(ks[3], (N_POOL_LAYERS, N_POOL_GROUPS, POOL_GROUP, POOL_GROUP), f32) * POOL_GROUP ** -0.5
    pool_scale = 1.0 + 0.1 * jax.random.normal(ks[4], (N_POOL_LAYERS, D_INNER), f32)
    pool_w_out = jax.random.normal(ks[5], (N_POOL_LAYERS, D_INNER, D_MODEL), f32) * D_INNER ** -0.5
    hgrn_w_in = jax.random.normal(ks[6], (N_HGRN_LAYERS, D_MODEL, 4 * D_INNER), f32) * D_MODEL ** -0.5
    hgrn_lower_bounds = 0.5 * jax.random.normal(ks[7], (DEPTH, D_INNER), f32)
    hgrn_norm_g = 1.0 + 0.05 * jax.random.normal(ks[8], (N_HGRN_LAYERS, D_INNER), f32)
    hgrn_w_out = jax.random.normal(ks[9], (N_HGRN_LAYERS, D_INNER, D_MODEL), f32) * D_INNER ** -0.5
    final_g = 1.0 + 0.05 * jax.random.normal(ks[10], (D_MODEL,), f32)
    return {"x": x, "norm_g": norm_g, "pool_w_in": pool_w_in, "pool_w_grp": pool_w_grp,
            "pool_scale": pool_scale, "pool_w_out": pool_w_out, "hgrn_w_in": hgrn_w_in,
            "hgrn_lower_bounds": hgrn_lower_bounds, "hgrn_norm_g": hgrn_norm_g,
            "hgrn_w_out": hgrn_w_out, "final_g": final_g}


def reference(x, norm_g, pool_w_in, pool_w_grp, pool_scale, pool_w_out, hgrn_w_in,
              hgrn_lower_bounds, hgrn_norm_g, hgrn_w_out, final_g):
    p = jax.nn.softmax(hgrn_lower_bounds.astype(jnp.float32), axis=0)
    lbs = jnp.cumsum(p, axis=0) - p[0:1]
    h = x
    for layer in range(DEPTH):
        hn = rms_norm(h, norm_g[layer])
        j = layer // N_MIXERS
        if layer % N_MIXERS == 0:
            y = pool_mixer(hn, pool_w_in[j], pool_w_grp[j], pool_scale[j], pool_w_out[j])
        else:
            y = hgrn2_mixer(hn, hgrn_w_in[j], lbs[layer], hgrn_norm_g[j], hgrn_w_out[j])
        h = h + y.astype(h.dtype)
    return rms_norm(h, final_g)
```

```python
import functools

import jax
import jax.numpy as jnp
from jax import lax
from jax.experimental import pallas as pl
from jax.experimental.pallas import tpu as pltpu

EPS = 1e-6
CHUNK = 64
HEAD_DIM = 128
POOL_WINDOWS = (2, 4, 8, 16)
HALO = 16
N_MIXERS = 2

V7X_VMEM_BYTES = 64 * 1024 * 1024
V7X_LANES = 128
V7X_SUBLANES = 8

F32 = jnp.float32
BF16 = jnp.bfloat16


def _rms_norm(x, g):
    ms = jnp.mean(x * x, axis=-1, keepdims=True)
    return x * lax.rsqrt(ms + EPS) * g


def _sigmoid(x):
    return 1.0 / (1.0 + jnp.exp(-x))


def _dot(a, b):
    return jnp.dot(a, b, preferred_element_type=F32)


def _dot_nt(a, b):
    return lax.dot_general(a, b, (((1,), (1,)), ((), ())), preferred_element_type=F32)


def _dot_tn(a, b):
    return lax.dot_general(a, b, (((0,), (0,)), ((), ())), preferred_element_type=F32)


def _pool_kernel(x_ref, g_ref, win_ref, wgrp_ref, scale_ref, wout_ref, o_ref,
                 carry_ref, y_ref, *, tm, d_inner):
    t = pl.program_id(1)
    gsz = d_inner // len(POOL_WINDOWS)

    @pl.when(t == 0)
    def _():
        carry_ref[...] = jnp.zeros_like(carry_ref)

    x = x_ref[0]
    hn = _rms_norm(x, g_ref[...]).astype(BF16)
    pos = t * tm + lax.broadcasted_iota(jnp.int32, (tm, 1), 0)

    for g, w in enumerate(POOL_WINDOWS):
        cols = slice(g * gsz, (g + 1) * gsz)
        v = _dot(hn, win_ref[:, cols])
        gate = _dot(hn, win_ref[:, d_inner + g * gsz:d_inner + (g + 1) * gsz])
        s = jnp.concatenate([carry_ref[:, cols], v], axis=0)
        carry_ref[:, cols] = v[tm - HALO:, :]
        shift = 1
        while shift < w:
            s = s + pltpu.roll(s, shift, axis=0)
            shift *= 2
        inv_cnt = 1.0 / jnp.minimum(pos + 1, w).astype(F32)
        p = s[HALO:, :] * inv_cnt - v
        m = _dot(p.astype(BF16), wgrp_ref[g]) * scale_ref[:, cols]
        y_ref[:, cols] = (m * (gate * _sigmoid(gate))).astype(BF16)

    o_ref[0] = x + _dot(y_ref[...], wout_ref[...])


def _pool_layer(x, g, w_in, w_grp, scale, w_out, *, tm):
    bsz, seq, d_model = x.shape
    d_inner = w_out.shape[0]
    const2 = lambda b, t: (0, 0)
    const3 = lambda b, t: (0, 0, 0)
    single = pl.Buffered(1)
    kern = functools.partial(_pool_kernel, tm=tm, d_inner=d_inner)
    return pl.pallas_call(
        kern,
        grid=(bsz, seq // tm),
        in_specs=[
            pl.BlockSpec((1, tm, d_model), lambda b, t: (b, t, 0)),
            pl.BlockSpec((1, d_model), const2, pipeline_mode=single),
            pl.BlockSpec(w_in.shape, const2, pipeline_mode=single),
            pl.BlockSpec(w_grp.shape, const3, pipeline_mode=single),
            pl.BlockSpec((1, d_inner), const2, pipeline_mode=single),
            pl.BlockSpec(w_out.shape, const2, pipeline_mode=single),
        ],
        out_specs=pl.BlockSpec((1, tm, d_model), lambda b, t: (b, t, 0)),
        out_shape=jax.ShapeDtypeStruct(x.shape, x.dtype),
        scratch_shapes=[
            pltpu.VMEM((HALO, d_inner), F32),
            pltpu.VMEM((tm, d_inner), BF16),
        ],
        compiler_params=pltpu.CompilerParams(
            dimension_semantics=("arbitrary", "arbitrary"),
            vmem_limit_bytes=48 * 1024 * 1024,
        ),
        name="pool_layer",
    )(x, g, w_in, w_grp, scale, w_out)


LEVELS_ROBUST = (("half", 64), ("half", 32), ("half", 16), ("half", 8), ("half", 4),
                 ("diag", 2))


def _block_ref_rows(b, b_ref, kind, n):
    def bcast(idx, rows):
        return jnp.broadcast_to(b_ref[idx:idx + 1, :], (rows, HEAD_DIM))

    if kind == "half" and n >= V7X_SUBLANES:
        return jnp.concatenate(
            [bcast(j * n + n // 2 - 1, n) for j in range(CHUNK // n)], axis=0)
    if kind == "half" and n == 4:
        sub = lax.broadcasted_iota(jnp.int32, (V7X_SUBLANES, HEAD_DIM), 0)
        pieces = []
        for r0 in range(0, CHUNK, V7X_SUBLANES):
            pieces.append(jnp.where(sub < 4, bcast(r0 + 1, V7X_SUBLANES),
                                    bcast(r0 + 5, V7X_SUBLANES)))
        return jnp.concatenate(pieces, axis=0)
    if kind == "diag" and n == 2:
        row = lax.broadcasted_iota(jnp.int32, (CHUNK, HEAD_DIM), 0)
        return jnp.where(row % 2 == 0, b, pltpu.roll(b, 1, axis=0))
    raise NotImplementedError((kind, n))


def _level_mask(kind, n):
    t = lax.broadcasted_iota(jnp.int32, (CHUNK, CHUNK), 0)
    s = lax.broadcasted_iota(jnp.int32, (CHUNK, CHUNK), 1)
    same = (t // n) == (s // n)
    if kind == "half":
        return same & ((t % n) >= n // 2) & ((s % n) < n // 2)
    return same & (t >= s)


def _hgrn_kernel(x_ref, g_ref, win_ref, lbp_ref, ng_ref, wout_ref, fg_ref, o_ref,
                 u_ref, st_ref, ob_ref, b_ref, *, tm, d_inner, layer, levels):
    t = pl.program_id(1)
    n_heads = d_inner // HEAD_DIM
    n_chunks = tm // CHUNK
    nb = 4 * HEAD_DIM

    @pl.when(t == 0)
    def _():
        st_ref[...] = jnp.zeros_like(st_ref)

    x = x_ref[0]
    hn = _rms_norm(x, g_ref[...]).astype(BF16)

    for sec in range(4):
        for blk in range(d_inner // nb):
            c0 = sec * d_inner + blk * nb
            u = _dot(hn, win_ref[:, c0:c0 + nb])
            for j in range(nb // HEAD_DIM):
                u_ref[sec, blk * (nb // HEAD_DIM) + j] = u[:, j * HEAD_DIM:(j + 1) * HEAD_DIM]

    row = lax.broadcasted_iota(jnp.int32, (CHUNK, CHUNK), 0)
    col = lax.broadcasted_iota(jnp.int32, (CHUNK, CHUNK), 1)
    tril = (row >= col).astype(BF16)

    def head_body(h, carry):
        lbp = lbp_ref[:, h].astype(F32)
        e = jnp.exp(lbp - jnp.max(lbp, axis=0, keepdims=True))
        p = e / jnp.sum(e, axis=0, keepdims=True)
        lb = jnp.sum(p[:layer + 1], axis=0) - p[0]
        one_m_lb = 1.0 - lb
        ng = ng_ref[h]
        st = st_ref[h]

        for c in range(n_chunks):
            rows = slice(c * CHUNK, (c + 1) * CHUNK)
            q = u_ref[0, h, rows, :]
            fz = u_ref[1, h, rows, :]
            vb = u_ref[2, h, rows, :].astype(BF16)
            gate = u_ref[3, h, rows, :]

            ez = jnp.exp(-jnp.abs(fz))
            rz = 1.0 / (1.0 + ez)
            pos = fz >= 0
            sig = jnp.where(pos, rz, ez * rz)
            sig_neg = jnp.where(pos, ez * rz, rz)
            f = lb + one_m_lb * sig
            logf = jnp.log(f)
            k = one_m_lb * sig_neg

            hi = logf.astype(BF16)
            r1 = logf - hi.astype(F32)
            mid = r1.astype(BF16)
            lo = (r1 - mid.astype(F32)).astype(BF16)
            b = _dot(tril, hi) + _dot(tril, mid) + _dot(tril, lo)
            b_ref[c] = b
            bc_ref = b_ref.at[c]

            a = jnp.zeros((CHUNK, CHUNK), F32)
            for kind, n in levels:
                ref = _block_ref_rows(b, bc_ref, kind, n)
                if kind == "half":
                    w = jnp.exp(-jnp.abs(b - ref))
                    wq, wk = w, w
                else:
                    wq = jnp.exp(b - ref)
                    wk = jnp.exp(ref - b)
                pl_ = _dot_nt((q * wq).astype(BF16), (k * wk).astype(BF16))
                a = jnp.where(_level_mask(kind, n), pl_, a)

            o = _dot(a.astype(BF16), vb)
            o = o + _dot_nt((q * jnp.exp(b)).astype(BF16), st.astype(BF16))
            b_last = b[CHUNK - 1:CHUNK, :]
            kd = (k * jnp.exp(b_last - b)).astype(BF16)
            st = st * jnp.exp(b_last) + _dot_tn(vb, kd)

            ms = jnp.mean(o * o, axis=-1, keepdims=True)
            on = o * lax.rsqrt(ms + EPS) * ng
            ob_ref[h, rows, :] = (on * (gate * _sigmoid(gate))).astype(BF16)

        st_ref[h] = st
        return carry

    lax.fori_loop(0, n_heads, head_body, 0)

    og = jnp.concatenate([ob_ref[h] for h in range(n_heads)], axis=1)
    h2 = x + _dot(og, wout_ref[...])
    o_ref[0] = _rms_norm(h2, fg_ref[...])


def _hgrn_layer(x, g, w_in, lb_params, norm_g, w_out, final_g, *, tm, layer, levels):
    bsz, seq, d_model = x.shape
    d_inner = w_out.shape[0]
    n_heads = d_inner // HEAD_DIM
    depth = lb_params.shape[0]
    const2 = lambda b, t: (0, 0)
    const3 = lambda b, t: (0, 0, 0)
    const4 = lambda b, t: (0, 0, 0, 0)
    single = pl.Buffered(1)
    kern = functools.partial(_hgrn_kernel, tm=tm, d_inner=d_inner, layer=layer,
                             levels=levels)
    return pl.pallas_call(
        kern,
        grid=(bsz, seq // tm),
        in_specs=[
            pl.BlockSpec((1, tm, d_model), lambda b, t: (b, t, 0)),
            pl.BlockSpec((1, d_model), const2, pipeline_mode=single),
            pl.BlockSpec(w_in.shape, const2, pipeline_mode=single),
            pl.BlockSpec((depth, n_heads, 1, HEAD_DIM), const4, pipeline_mode=single),
            pl.BlockSpec((n_heads, 1, HEAD_DIM), const3, pipeline_mode=single),
            pl.BlockSpec(w_out.shape, const2, pipeline_mode=single),
            pl.BlockSpec((1, d_model), const2, pipeline_mode=single),
        ],
        out_specs=pl.BlockSpec((1, tm, d_model), lambda b, t: (b, t, 0)),
        out_shape=jax.ShapeDtypeStruct(x.shape, x.dtype),
        scratch_shapes=[
            pltpu.VMEM((4, n_heads, tm, HEAD_DIM), F32),
            pltpu.VMEM((n_heads, HEAD_DIM, HEAD_DIM), F32),
            pltpu.VMEM((n_heads, tm, HEAD_DIM), BF16),
            pltpu.VMEM((tm // CHUNK, CHUNK, HEAD_DIM), F32),
        ],
        compiler_params=pltpu.CompilerParams(
            dimension_semantics=("arbitrary", "arbitrary"),
            vmem_limit_bytes=56 * 1024 * 1024,
        ),
        name="hgrn_layer",
    )(x, g, w_in, lb_params.reshape(depth, n_heads, 1, HEAD_DIM),
      norm_g.reshape(n_heads, 1, HEAD_DIM), w_out, final_g)


def kernel(x, norm_g, pool_w_in, pool_w_grp, pool_scale, pool_w_out, hgrn_w_in,
           hgrn_lower_bounds, hgrn_norm_g, hgrn_w_out, final_g):
    depth = norm_g.shape[0]
    assert depth == 2 and pool_w_in.shape[0] == 1 and hgrn_w_in.shape[0] == 1
    h = _pool_layer(
        x, norm_g[0:1], pool_w_in[0].astype(BF16), pool_w_grp[0].astype(BF16),
        pool_scale[0:1], pool_w_out[0].astype(BF16), tm=256)
    return _hgrn_layer(
        h, norm_g[1:2], hgrn_w_in[0].astype(BF16), hgrn_lower_bounds, hgrn_norm_g[0],
        hgrn_w_out[0].astype(BF16), final_g.reshape(1, -1), tm=256, layer=1,
        levels=LEVELS_ROBUST)
```

```python
import functools

import jax
import jax.numpy as jnp
from jax import lax
from jax.experimental import pallas as pl
from jax.experimental.pallas import tpu as pltpu

EPS = 1e-6
CHUNK = 64
HEAD_DIM = 128
HEADS_PER_STEP = 2
POOL_WINDOWS = (2, 4, 8, 16)
HALO = 16
N_MIXERS = 2

V7X_VMEM_BYTES = 64 * 1024 * 1024
V7X_LANES = 128
V7X_SUBLANES = 8

F32 = jnp.float32
BF16 = jnp.bfloat16


def _rms_norm(x, g):
    ms = jnp.mean(x * x, axis=-1, keepdims=True)
    return x * lax.rsqrt(ms + EPS) * g


def _sigmoid(x):
    return 1.0 / (1.0 + jnp.exp(-x))


def _dot(a, b):
    return jnp.dot(a, b, preferred_element_type=F32)


def _dot_nt(a, b):
    return lax.dot_general(a, b, (((1,), (1,)), ((), ())), preferred_element_type=F32)


def _dot_tn(a, b):
    return lax.dot_general(a, b, (((0,), (0,)), ((), ())), preferred_element_type=F32)


def _pool_kernel(x_ref, g_ref, win_ref, wgrp_ref, scale_ref, wout_ref, o_ref,
                 carry_ref, y_ref, *, tm, d_inner):
    t = pl.program_id(1)
    gsz = d_inner // len(POOL_WINDOWS)

    @pl.when(t == 0)
    def _():
        carry_ref[...] = jnp.zeros_like(carry_ref)

    x = x_ref[0]
    hn = _rms_norm(x, g_ref[...]).astype(BF16)
    pos = t * tm + lax.broadcasted_iota(jnp.int32, (tm, 1), 0)

    for g, w in enumerate(POOL_WINDOWS):
        cols = slice(g * gsz, (g + 1) * gsz)
        v = _dot(hn, win_ref[:, cols])
        gate = _dot(hn, win_ref[:, d_inner + g * gsz:d_inner + (g + 1) * gsz])
        s = jnp.concatenate([carry_ref[:, cols], v], axis=0)
        carry_ref[:, cols] = v[tm - HALO:, :]
        shift = 1
        while shift < w:
            s = s + pltpu.roll(s, shift, axis=0)
            shift *= 2
        inv_cnt = 1.0 / jnp.minimum(pos + 1, w).astype(F32)
        p = s[HALO:, :] * inv_cnt - v
        m = _dot(p.astype(BF16), wgrp_ref[g]) * scale_ref[:, cols]
        y_ref[:, cols] = (m * (gate * _sigmoid(gate))).astype(BF16)

    o_ref[0] = x + _dot(y_ref[...], wout_ref[...])


def _pool_layer(x, g, w_in, w_grp, scale, w_out, *, tm):
    bsz, seq, d_model = x.shape
    d_inner = w_out.shape[0]
    const2 = lambda b, t: (0, 0)
    const3 = lambda b, t: (0, 0, 0)
    single = pl.Buffered(1)
    kern = functools.partial(_pool_kernel, tm=tm, d_inner=d_inner)
    return pl.pallas_call(
        kern,
        grid=(bsz, seq // tm),
        in_specs=[
            pl.BlockSpec((1, tm, d_model), lambda b, t: (b, t, 0)),
            pl.BlockSpec((1, d_model), const2, pipeline_mode=single),
            pl.BlockSpec(w_in.shape, const2, pipeline_mode=single),
            pl.BlockSpec(w_grp.shape, const3, pipeline_mode=single),
            pl.BlockSpec((1, d_inner), const2, pipeline_mode=single),
            pl.BlockSpec(w_out.shape, const2, pipeline_mode=single),
        ],
        out_specs=pl.BlockSpec((1, tm, d_model), lambda b, t: (b, t, 0)),
        out_shape=jax.ShapeDtypeStruct(x.shape, x.dtype),
        scratch_shapes=[
            pltpu.VMEM((HALO, d_inner), F32),
            pltpu.VMEM((tm, d_inner), BF16),
        ],
        compiler_params=pltpu.CompilerParams(
            dimension_semantics=("arbitrary", "arbitrary"),
            vmem_limit_bytes=48 * 1024 * 1024,
        ),
        name="pool_layer",
    )(x, g, w_in, w_grp, scale, w_out)


LEVELS_ROBUST = (("half", 64), ("half", 32), ("half", 16), ("half", 8), ("half", 4),
                 ("pair", 2))


def _neg_abs(x):
    bits = lax.bitcast_convert_type(x, jnp.uint32) | jnp.uint32(0x80000000)
    return lax.bitcast_convert_type(bits, F32)


def _half_level_operand(q, k, b, b_ref, n):
    lanes = b.shape[1]
    half = n // 2

    def ref_row(idx, rows):
        return jnp.broadcast_to(b_ref[idx:idx + 1, :], (rows, lanes))

    if half >= V7X_SUBLANES:
        pieces = []
        for j in range(CHUNK // n):
            lo, mid, hi = j * n, j * n + half, (j + 1) * n
            ref = ref_row(mid - 1, half)
            pieces.append(k[lo:mid] * jnp.exp2(ref - b[lo:mid]))
            pieces.append(q[mid:hi] * jnp.exp2(b[mid:hi] - ref))
        return jnp.concatenate(pieces, axis=0)

    sub = lax.broadcasted_iota(jnp.int32, (V7X_SUBLANES, lanes), 0)
    if n == 8:
        ref = jnp.concatenate(
            [ref_row(r0 + 3, V7X_SUBLANES) for r0 in range(0, CHUNK, V7X_SUBLANES)], axis=0)
    elif n == 4:
        ref = jnp.concatenate(
            [jnp.where(sub < 4, ref_row(r0 + 1, V7X_SUBLANES), ref_row(r0 + 5, V7X_SUBLANES))
             for r0 in range(0, CHUNK, V7X_SUBLANES)], axis=0)
    else:
        raise NotImplementedError(n)
    row = lax.broadcasted_iota(jnp.int32, b.shape, 0)
    qk = jnp.where((row % n) >= half, q, k)
    return qk * jnp.exp2(_neg_abs(b - ref))


def _level_mask(kind, n):
    t = lax.broadcasted_iota(jnp.int32, (CHUNK, CHUNK), 0)
    s = lax.broadcasted_iota(jnp.int32, (CHUNK, CHUNK), 1)
    same = (t // n) == (s // n)
    if kind == "half":
        return same & ((t % n) >= n // 2) & ((s % n) < n // 2)
    return same & (t >= s)


def _hgrn_kernel(x_ref, g_ref, win_ref, lbp_ref, ng_ref, wout_ref, fg_ref, o_ref,
                 u_ref, st_ref, ob_ref, b_ref, *, tm, d_inner, layer, levels):
    t = pl.program_id(1)
    hps = HEADS_PER_STEP
    gw = hps * HEAD_DIM
    n_groups = d_inner // gw
    n_chunks = tm // CHUNK

    @pl.when(t == 0)
    def _():
        st_ref[...] = jnp.zeros_like(st_ref)

    x = x_ref[0]
    hn = _rms_norm(x, g_ref[...]).astype(BF16)

    for g in range(n_groups):
        u_ref[g] = _dot(hn, win_ref[g])

    row = lax.broadcasted_iota(jnp.int32, (CHUNK, CHUNK), 0)
    col = lax.broadcasted_iota(jnp.int32, (CHUNK, CHUNK), 1)
    tril = (row >= col).astype(BF16)
    masks = [_level_mask(kind, n) for kind, n in levels]
    odd = (lax.broadcasted_iota(jnp.int32, (CHUNK, gw), 0) % 2) == 1

    def group_body(g, carry):
        lbp = lbp_ref[:, g].astype(F32)
        e = jnp.exp(lbp - jnp.max(lbp, axis=0, keepdims=True))
        p = e / jnp.sum(e, axis=0, keepdims=True)
        lb = jnp.sum(p[:layer + 1], axis=0) - p[0]
        one_m_lb = 1.0 - lb
        ng = ng_ref[g]
        st = [st_ref[g * hps + j] for j in range(hps)]

        for c in range(n_chunks):
            rows = slice(c * CHUNK, (c + 1) * CHUNK)
            q = u_ref[g, rows, 0 * gw:1 * gw]
            fz = u_ref[g, rows, 1 * gw:2 * gw]
            vb = u_ref[g, rows, 2 * gw:3 * gw].astype(BF16)
            gate = u_ref[g, rows, 3 * gw:4 * gw]

            ez = jnp.exp(-jnp.abs(fz))
            rz = 1.0 / (1.0 + ez)
            pos = fz >= 0
            sig = jnp.where(pos, rz, ez * rz)
            sig_neg = jnp.where(pos, ez * rz, rz)
            f = lb + one_m_lb * sig
            lf = jnp.log2(f)
            k = one_m_lb * sig_neg

            hi = lf.astype(BF16)
            r1 = lf - hi.astype(F32)
            mid = r1.astype(BF16)
            lo = (r1 - mid.astype(F32)).astype(BF16)
            cs = _dot(tril, jnp.concatenate([hi, mid, lo], axis=1))
            b = cs[:, 0:gw] + cs[:, gw:2 * gw] + cs[:, 2 * gw:3 * gw]
            b_ref[c] = b
            bc_ref = b_ref.at[c]

            a = [jnp.zeros((CHUNK, CHUNK), F32) for _ in range(hps)]
            for (kind, n), mask in zip(levels, masks):
                if kind == "half":
                    xq = _half_level_operand(q, k, b, bc_ref, n).astype(BF16)
                    xk = xq
                else:
                    xq = jnp.where(odd, q * f, q).astype(BF16)
                    xk = jnp.where(odd, k * (1.0 / f), k).astype(BF16)
                for j in range(hps):
                    hl = slice(j * HEAD_DIM, (j + 1) * HEAD_DIM)
                    a[j] = jnp.where(mask, _dot_nt(xq[:, hl], xk[:, hl]), a[j])

            qe = (q * jnp.exp2(b)).astype(BF16)
            b_last = b[CHUNK - 1:CHUNK, :]
            kd = (k * jnp.exp2(b_last - b)).astype(BF16)
            decay = jnp.exp2(b_last)
            for j in range(hps):
                hl = slice(j * HEAD_DIM, (j + 1) * HEAD_DIM)
                o = _dot(a[j].astype(BF16), vb[:, hl])
                o = o + _dot_nt(qe[:, hl], st[j].astype(BF16))
                st[j] = st[j] * decay[:, hl] + _dot_tn(vb[:, hl], kd[:, hl])
                ms = jnp.mean(o * o, axis=-1, keepdims=True)
                on = o * lax.rsqrt(ms + EPS) * ng[:, hl]
                gt = gate[:, hl]
                ob_ref[g, rows, hl] = (on * (gt * _sigmoid(gt))).astype(BF16)

        for j in range(hps):
            st_ref[g * hps + j] = st[j]
        return carry

    lax.fori_loop(0, n_groups, group_body, 0)

    og = jnp.concatenate([ob_ref[g] for g in range(n_groups)], axis=1)
    h2 = x + _dot(og, wout_ref[...])
    o_ref[0] = _rms_norm(h2, fg_ref[...])


def _hgrn_layer(x, g, w_in, lb_params, norm_g, w_out, final_g, *, tm, layer, levels):
    bsz, seq, d_model = x.shape
    d_inner = w_out.shape[0]
    n_heads = d_inner // HEAD_DIM
    gw = HEADS_PER_STEP * HEAD_DIM
    n_groups = d_inner // gw
    depth = lb_params.shape[0]
    w_grouped = (w_in.reshape(d_model, 4, n_groups, gw).transpose(2, 0, 1, 3)
                 .reshape(n_groups, d_model, 4 * gw))
    const2 = lambda b, t: (0, 0)
    const3 = lambda b, t: (0, 0, 0)
    const4 = lambda b, t: (0, 0, 0, 0)
    single = pl.Buffered(1)
    kern = functools.partial(_hgrn_kernel, tm=tm, d_inner=d_inner, layer=layer,
                             levels=levels)
    return pl.pallas_call(
        kern,
        grid=(bsz, seq // tm),
        in_specs=[
            pl.BlockSpec((1, tm, d_model), lambda b, t: (b, t, 0)),
            pl.BlockSpec((1, d_model), const2, pipeline_mode=single),
            pl.BlockSpec(w_grouped.shape, const3, pipeline_mode=single),
            pl.BlockSpec((depth, n_groups, 1, gw), const4, pipeline_mode=single),
            pl.BlockSpec((n_groups, 1, gw), const3, pipeline_mode=single),
            pl.BlockSpec(w_out.shape, const2, pipeline_mode=single),
            pl.BlockSpec((1, d_model), const2, pipeline_mode=single),
        ],
        out_specs=pl.BlockSpec((1, tm, d_model), lambda b, t: (b, t, 0)),
        out_shape=jax.ShapeDtypeStruct(x.shape, x.dtype),
        scratch_shapes=[
            pltpu.VMEM((n_groups, tm, 4 * gw), F32),
            pltpu.VMEM((n_heads, HEAD_DIM, HEAD_DIM), F32),
            pltpu.VMEM((n_groups, tm, gw), BF16),
            pltpu.VMEM((tm // CHUNK, CHUNK, gw), F32),
        ],
        compiler_params=pltpu.CompilerParams(
            dimension_semantics=("arbitrary", "arbitrary"),
            vmem_limit_bytes=56 * 1024 * 1024,
        ),
        name="hgrn_layer",
    )(x, g, w_grouped, lb_params.reshape(depth, n_groups, 1, gw),
      norm_g.reshape(n_groups, 1, gw), w_out, final_g)


def kernel(x, norm_g, pool_w_in, pool_w_grp, pool_scale, pool_w_out, hgrn_w_in,
           hgrn_lower_bounds, hgrn_norm_g, hgrn_w_out, final_g):
    depth = norm_g.shape[0]
    assert depth == 2 and pool_w_in.shape[0] == 1 and hgrn_w_in.shape[0] == 1
    h = _pool_layer(
        x, norm_g[0:1], pool_w_in[0].astype(BF16), pool_w_grp[0].astype(BF16),
        pool_scale[0:1], pool_w_out[0].astype(BF16), tm=256)
    return _hgrn_layer(
        h, norm_g[1:2], hgrn_w_in[0].astype(BF16), hgrn_lower_bounds, hgrn_norm_g[0],
        hgrn_w_out[0].astype(BF16), final_g.reshape(1, -1), tm=256, layer=1,
        levels=LEVELS_ROBUST)
```

```python
import functools

import jax
import jax.numpy as jnp
from jax import lax
from jax.experimental import pallas as pl
from jax.experimental.pallas import tpu as pltpu

EPS = 1e-6
CHUNK = 64
HEAD_DIM = 128
HEADS_PER_STEP = 2
POOL_WINDOWS = (2, 4, 8, 16)
HALO = 16
N_MIXERS = 2

V7X_VMEM_BYTES = 64 * 1024 * 1024
V7X_LANES = 128
V7X_SUBLANES = 8

F32 = jnp.float32
BF16 = jnp.bfloat16


def _rms_norm(x, g):
    ms = jnp.mean(x * x, axis=-1, keepdims=True)
    return x * lax.rsqrt(ms + EPS) * g


def _sigmoid(x):
    return 1.0 / (1.0 + jnp.exp(-x))


def _dot(a, b):
    return jnp.dot(a, b, preferred_element_type=F32)


def _dot_nt(a, b):
    return lax.dot_general(a, b, (((1,), (1,)), ((), ())), preferred_element_type=F32)


def _dot_tn(a, b):
    return lax.dot_general(a, b, (((0,), (0,)), ((), ())), preferred_element_type=F32)


def _pool_kernel(x_ref, g_ref, win_ref, wgrp_ref, scale_ref, wout_ref, o_ref,
                 carry_ref, y_ref, *, tm, d_inner):
    t = pl.program_id(1)
    gsz = d_inner // len(POOL_WINDOWS)

    @pl.when(t == 0)
    def _():
        carry_ref[...] = jnp.zeros_like(carry_ref)

    x = x_ref[0]
    hn = _rms_norm(x, g_ref[...]).astype(BF16)
    pos = t * tm + lax.broadcasted_iota(jnp.int32, (tm, 1), 0)

    for g, w in enumerate(POOL_WINDOWS):
        cols = slice(g * gsz, (g + 1) * gsz)
        v = _dot(hn, win_ref[:, cols])
        gate = _dot(hn, win_ref[:, d_inner + g * gsz:d_inner + (g + 1) * gsz])
        s = jnp.concatenate([carry_ref[:, cols], v], axis=0)
        carry_ref[:, cols] = v[tm - HALO:, :]
        shift = 1
        while shift < w:
            s = s + pltpu.roll(s, shift, axis=0)
            shift *= 2
        inv_cnt = 1.0 / jnp.minimum(pos + 1, w).astype(F32)
        p = s[HALO:, :] * inv_cnt - v
        m = _dot(p.astype(BF16), wgrp_ref[g]) * scale_ref[:, cols]
        y_ref[:, cols] = (m * (gate * _sigmoid(gate))).astype(BF16)

    o_ref[0] = x + _dot(y_ref[...], wout_ref[...])


def _pool_layer(x, g, w_in, w_grp, scale, w_out, *, tm):
    bsz, seq, d_model = x.shape
    d_inner = w_out.shape[0]
    const2 = lambda b, t: (0, 0)
    const3 = lambda b, t: (0, 0, 0)
    single = pl.Buffered(1)
    kern = functools.partial(_pool_kernel, tm=tm, d_inner=d_inner)
    return pl.pallas_call(
        kern,
        grid=(bsz, seq // tm),
        in_specs=[
            pl.BlockSpec((1, tm, d_model), lambda b, t: (b, t, 0)),
            pl.BlockSpec((1, d_model), const2, pipeline_mode=single),
            pl.BlockSpec(w_in.shape, const2, pipeline_mode=single),
            pl.BlockSpec(w_grp.shape, const3, pipeline_mode=single),
            pl.BlockSpec((1, d_inner), const2, pipeline_mode=single),
            pl.BlockSpec(w_out.shape, const2, pipeline_mode=single),
        ],
        out_specs=pl.BlockSpec((1, tm, d_model), lambda b, t: (b, t, 0)),
        out_shape=jax.ShapeDtypeStruct(x.shape, x.dtype),
        scratch_shapes=[
            pltpu.VMEM((HALO, d_inner), F32),
            pltpu.VMEM((tm, d_inner), BF16),
        ],
        compiler_params=pltpu.CompilerParams(
            dimension_semantics=("arbitrary", "arbitrary"),
            vmem_limit_bytes=48 * 1024 * 1024,
        ),
        name="pool_layer",
    )(x, g, w_in, w_grp, scale, w_out)


LEVELS_ROBUST = (("half", 64), ("half", 32), ("half", 16), ("half", 8), ("half", 4),
                 ("pair", 2))


def _half_level_operand(q, k, b, n):
    lanes = b.shape[1]
    half = n // 2

    def ref_row(idx, rows):
        return jnp.broadcast_to(b[idx:idx + 1, :], (rows, lanes))

    if half >= V7X_SUBLANES:
        pieces = []
        for j in range(CHUNK // n):
            lo, mid, hi = j * n, j * n + half, (j + 1) * n
            ref = ref_row(mid - 1, half)
            pieces.append(k[lo:mid] * jnp.exp2(ref - b[lo:mid]))
            pieces.append(q[mid:hi] * jnp.exp2(b[mid:hi] - ref))
        return jnp.concatenate(pieces, axis=0)

    sub = lax.broadcasted_iota(jnp.int32, (V7X_SUBLANES, lanes), 0)
    if n == 8:
        ref = jnp.concatenate(
            [ref_row(r0 + 3, V7X_SUBLANES) for r0 in range(0, CHUNK, V7X_SUBLANES)], axis=0)
    elif n == 4:
        ref = jnp.concatenate(
            [jnp.where(sub < 4, ref_row(r0 + 1, V7X_SUBLANES), ref_row(r0 + 5, V7X_SUBLANES))
             for r0 in range(0, CHUNK, V7X_SUBLANES)], axis=0)
    else:
        raise NotImplementedError(n)
    row = lax.broadcasted_iota(jnp.int32, b.shape, 0)
    qk = jnp.where((row % n) >= half, q, k)
    return qk * jnp.exp2(-jnp.abs(b - ref))


def _level_mask(kind, n):
    t = lax.broadcasted_iota(jnp.int32, (CHUNK, CHUNK), 0)
    s = lax.broadcasted_iota(jnp.int32, (CHUNK, CHUNK), 1)
    same = (t // n) == (s // n)
    if kind == "half":
        return same & ((t % n) >= n // 2) & ((s % n) < n // 2)
    return same & (t >= s)


def _hgrn_kernel(x_ref, g_ref, win_ref, lbp_ref, ng_ref, wout_ref, fg_ref, o_ref,
                 hn_ref, u_even, u_odd, st_even, st_odd, ob_ref, *,
                 tm, d_inner, layer, levels):
    t = pl.program_id(1)
    hps = HEADS_PER_STEP
    gw = hps * HEAD_DIM
    n_groups = d_inner // gw
    n_chunks = tm // CHUNK
    assert n_groups % 2 == 0

    @pl.when(t == 0)
    def _():
        st_even[...] = jnp.zeros_like(st_even)
        st_odd[...] = jnp.zeros_like(st_odd)

    x = x_ref[0]
    hn_ref[...] = _rms_norm(x, g_ref[...]).astype(BF16)

    row = lax.broadcasted_iota(jnp.int32, (CHUNK, CHUNK), 0)
    col = lax.broadcasted_iota(jnp.int32, (CHUNK, CHUNK), 1)
    tril = (row >= col).astype(BF16)
    masks = [_level_mask(kind, n) for kind, n in levels]
    odd = (lax.broadcasted_iota(jnp.int32, (CHUNK, gw), 0) % 2) == 1

    def project_piece(g, u_ref, sec):
        cols = slice(sec * gw, (sec + 1) * gw)
        u_ref[:, cols] = _dot(hn_ref[...], win_ref[g, :, cols])

    def group_consts(g):
        lbp = lbp_ref[:, g].astype(F32)
        e = jnp.exp(lbp - jnp.max(lbp, axis=0, keepdims=True))
        p = e / jnp.sum(e, axis=0, keepdims=True)
        lb = jnp.sum(p[:layer + 1], axis=0) - p[0]
        return lb, 1.0 - lb, ng_ref[g]

    def stage_gates(u_ref, c, lb, one_m_lb):
        rows = slice(c * CHUNK, (c + 1) * CHUNK)
        q = u_ref[rows, 0 * gw:1 * gw]
        fz = u_ref[rows, 1 * gw:2 * gw]
        ez = jnp.exp(-jnp.abs(fz))
        rz = 1.0 / (1.0 + ez)
        pos = fz >= 0
        sig = jnp.where(pos, rz, ez * rz)
        sig_neg = jnp.where(pos, ez * rz, rz)
        f = lb + one_m_lb * sig
        lf = jnp.log2(f)
        k = one_m_lb * sig_neg
        hi = lf.astype(BF16)
        r1 = lf - hi.astype(F32)
        mid = r1.astype(BF16)
        lo = (r1 - mid.astype(F32)).astype(BF16)
        cs = _dot(tril, jnp.concatenate([hi, mid, lo], axis=1))
        return q, k, f, cs

    def stage_intra(q, k, f, cs):
        b = cs[:, 0:gw] + cs[:, gw:2 * gw] + cs[:, 2 * gw:3 * gw]
        a = [jnp.zeros((CHUNK, CHUNK), F32) for _ in range(hps)]
        for (kind, n), mask in zip(levels, masks):
            if kind == "half":
                xq = _half_level_operand(q, k, b, n).astype(BF16)
                xk = xq
            else:
                xq = jnp.where(odd, q * f, q).astype(BF16)
                xk = jnp.where(odd, k * (1.0 / f), k).astype(BF16)
            for j in range(hps):
                hl = slice(j * HEAD_DIM, (j + 1) * HEAD_DIM)
                a[j] = jnp.where(mask, _dot_nt(xq[:, hl], xk[:, hl]), a[j])
        qe = (q * jnp.exp2(b)).astype(BF16)
        b_last = b[CHUNK - 1:CHUNK, :]
        kd = (k * jnp.exp2(b_last - b)).astype(BF16)
        return [aj.astype(BF16) for aj in a], qe, kd, jnp.exp2(b_last)

    def stage_output(g, u_ref, c, ng, st, a, qe, kd, decay):
        rows = slice(c * CHUNK, (c + 1) * CHUNK)
        vb = u_ref[rows, 2 * gw:3 * gw].astype(BF16)
        gate = u_ref[rows, 3 * gw:4 * gw]
        for j in range(hps):
            hl = slice(j * HEAD_DIM, (j + 1) * HEAD_DIM)
            o = _dot(a[j], vb[:, hl]) + _dot_nt(qe[:, hl], st[j].astype(BF16))
            st[j] = st[j] * decay[:, hl] + _dot_tn(vb[:, hl], kd[:, hl])
            ms = jnp.mean(o * o, axis=-1, keepdims=True)
            on = o * lax.rsqrt(ms + EPS) * ng[:, hl]
            gt = gate[:, hl]
            ob_ref[g, rows, hl] = (on * (gt * _sigmoid(gt))).astype(BF16)

    def run_groups(groups, pieces):
        items = [(gi, c) for gi in range(len(groups)) for c in range(n_chunks)]
        consts, states = {}, {}
        gates_out, intra_out = {}, {}
        for step in range(len(items) + 2):
            if step >= 2:
                gi, c = items[step - 2]
                g, pair, u_ref, st_ref = groups[gi]
                if c == 0:
                    states[gi] = [st_ref[pair * hps + j] for j in range(hps)]
                stage_output(g, u_ref, c, consts[gi][2], states[gi], *intra_out.pop((gi, c)))
                if c == n_chunks - 1:
                    for j in range(hps):
                        st_ref[pair * hps + j] = states[gi][j]
            if 1 <= step <= len(items):
                intra_out[items[step - 1]] = stage_intra(*gates_out.pop(items[step - 1]))
            if step < len(items):
                gi, c = items[step]
                g, pair, u_ref, st_ref = groups[gi]
                if c == 0:
                    consts[gi] = group_consts(g)
                gates_out[(gi, c)] = stage_gates(u_ref, c, consts[gi][0], consts[gi][1])
            if step < len(pieces) and pieces[step] is not None:
                project_piece(*pieces[step])

    n_pairs = n_groups // 2
    for sec in range(4):
        project_piece(0, u_even, sec)

    def pair_body(i, carry):
        run_groups(
            [(2 * i, i, u_even, st_even), (2 * i + 1, i, u_odd, st_odd)],
            [(2 * i + 1, u_odd, sec) for sec in range(4)]
            + [(2 * i + 2, u_even, sec) for sec in range(4)])
        return carry

    lax.fori_loop(0, n_pairs - 1, pair_body, 0)
    run_groups(
        [(n_groups - 2, n_pairs - 1, u_even, st_even),
         (n_groups - 1, n_pairs - 1, u_odd, st_odd)],
        [(n_groups - 1, u_odd, sec) for sec in range(4)])

    og = jnp.concatenate([ob_ref[g] for g in range(n_groups)], axis=1)
    h2 = x + _dot(og, wout_ref[...])
    o_ref[0] = _rms_norm(h2, fg_ref[...])


def _hgrn_layer(x, g, w_in, lb_params, norm_g, w_out, final_g, *, tm, layer, levels):
    bsz, seq, d_model = x.shape
    d_inner = w_out.shape[0]
    n_heads = d_inner // HEAD_DIM
    gw = HEADS_PER_STEP * HEAD_DIM
    n_groups = d_inner // gw
    depth = lb_params.shape[0]
    w_grouped = (w_in.reshape(d_model, 4, n_groups, gw).transpose(2, 0, 1, 3)
                 .reshape(n_groups, d_model, 4 * gw))
    const2 = lambda b, t: (0, 0)
    const3 = lambda b, t: (0, 0, 0)
    const4 = lambda b, t: (0, 0, 0, 0)
    single = pl.Buffered(1)
    kern = functools.partial(_hgrn_kernel, tm=tm, d_inner=d_inner, layer=layer,
                             levels=levels)
    return pl.pallas_call(
        kern,
        grid=(bsz, seq // tm),
        in_specs=[
            pl.BlockSpec((1, tm, d_model), lambda b, t: (b, t, 0)),
            pl.BlockSpec((1, d_model), const2, pipeline_mode=single),
            pl.BlockSpec(w_grouped.shape, const3, pipeline_mode=single),
            pl.BlockSpec((depth, n_groups, 1, gw), const4, pipeline_mode=single),
            pl.BlockSpec((n_groups, 1, gw), const3, pipeline_mode=single),
            pl.BlockSpec(w_out.shape, const2, pipeline_mode=single),
            pl.BlockSpec((1, d_model), const2, pipeline_mode=single),
        ],
        out_specs=pl.BlockSpec((1, tm, d_model), lambda b, t: (b, t, 0)),
        out_shape=jax.ShapeDtypeStruct(x.shape, x.dtype),
        scratch_shapes=[
            pltpu.VMEM((tm, d_model), BF16),
            pltpu.VMEM((tm, 4 * gw), F32),
            pltpu.VMEM((tm, 4 * gw), F32),
            pltpu.VMEM((n_heads // 2, HEAD_DIM, HEAD_DIM), F32),
            pltpu.VMEM((n_heads // 2, HEAD_DIM, HEAD_DIM), F32),
            pltpu.VMEM((n_groups, tm, gw), BF16),
        ],
        compiler_params=pltpu.CompilerParams(
            dimension_semantics=("arbitrary", "arbitrary"),
            vmem_limit_bytes=56 * 1024 * 1024,
        ),
        name="hgrn_layer",
    )(x, g, w_grouped, lb_params.reshape(depth, n_groups, 1, gw),
      norm_g.reshape(n_groups, 1, gw), w_out, final_g)


def kernel(x, norm_g, pool_w_in, pool_w_grp, pool_scale, pool_w_out, hgrn_w_in,
           hgrn_lower_bounds, hgrn_norm_g, hgrn_w_out, final_g):
    depth = norm_g.shape[0]
    assert depth == 2 and pool_w_in.shape[0] == 1 and hgrn_w_in.shape[0] == 1
    h = _pool_layer(
        x, norm_g[0:1], pool_w_in[0].astype(BF16), pool_w_grp[0].astype(BF16),
        pool_scale[0:1], pool_w_out[0].astype(BF16), tm=256)
    return _hgrn_layer(
        h, norm_g[1:2], hgrn_w_in[0].astype(BF16), hgrn_lower_bounds, hgrn_norm_g[0],
        hgrn_w_out[0].astype(BF16), final_g.reshape(1, -1), tm=256, layer=1,
        levels=LEVELS_ROBUST)
```

```python
import functools

import jax
import jax.numpy as jnp
from jax import lax
from jax.experimental import pallas as pl
from jax.experimental.pallas import tpu as pltpu

EPS = 1e-6
CHUNK = 64
HEAD_DIM = 128
HEADS_PER_STEP = 2
GROUPS_PER_BLOCK = 8
POOL_WINDOWS = (2, 4, 8, 16)
HALO = 16
N_MIXERS = 2

V7X_VMEM_BYTES = 64 * 1024 * 1024
V7X_LANES = 128
V7X_SUBLANES = 8

F32 = jnp.float32
BF16 = jnp.bfloat16


def _rms_norm(x, g):
    ms = jnp.mean(x * x, axis=-1, keepdims=True)
    return x * lax.rsqrt(ms + EPS) * g


def _sigmoid(x):
    return 1.0 / (1.0 + jnp.exp(-x))


def _dot(a, b):
    return jnp.dot(a, b, preferred_element_type=F32)


def _dot_nt(a, b):
    return lax.dot_general(a, b, (((1,), (1,)), ((), ())), preferred_element_type=F32)


def _dot_tn(a, b):
    return lax.dot_general(a, b, (((0,), (0,)), ((), ())), preferred_element_type=F32)


def _pool_kernel(x_ref, g_ref, win_ref, wgrp_ref, scale_ref, wout_ref, o_ref,
                 carry_ref, y_ref, *, tm, d_inner):
    t = pl.program_id(1)
    gsz = d_inner // len(POOL_WINDOWS)

    @pl.when(t == 0)
    def _():
        carry_ref[...] = jnp.zeros_like(carry_ref)

    x = x_ref[0]
    hn = _rms_norm(x, g_ref[...]).astype(BF16)
    pos = t * tm + lax.broadcasted_iota(jnp.int32, (tm, 1), 0)

    for g, w in enumerate(POOL_WINDOWS):
        cols = slice(g * gsz, (g + 1) * gsz)
        v = _dot(hn, win_ref[:, cols])
        gate = _dot(hn, win_ref[:, d_inner + g * gsz:d_inner + (g + 1) * gsz])
        s = jnp.concatenate([carry_ref[:, cols], v], axis=0)
        carry_ref[:, cols] = v[tm - HALO:, :]
        shift = 1
        while shift < w:
            s = s + pltpu.roll(s, shift, axis=0)
            shift *= 2
        inv_cnt = 1.0 / jnp.minimum(pos + 1, w).astype(F32)
        p = s[HALO:, :] * inv_cnt - v
        m = _dot(p.astype(BF16), wgrp_ref[g]) * scale_ref[:, cols]
        y_ref[:, cols] = (m * (gate * _sigmoid(gate))).astype(BF16)

    o_ref[0] = x + _dot(y_ref[...], wout_ref[...])


def _pool_layer(x, g, w_in, w_grp, scale, w_out, *, tm):
    bsz, seq, d_model = x.shape
    d_inner = w_out.shape[0]
    const2 = lambda b, t: (0, 0)
    const3 = lambda b, t: (0, 0, 0)
    single = pl.Buffered(1)
    kern = functools.partial(_pool_kernel, tm=tm, d_inner=d_inner)
    return pl.pallas_call(
        kern,
        grid=(bsz, seq // tm),
        in_specs=[
            pl.BlockSpec((1, tm, d_model), lambda b, t: (b, t, 0)),
            pl.BlockSpec((1, d_model), const2, pipeline_mode=single),
            pl.BlockSpec(w_in.shape, const2, pipeline_mode=single),
            pl.BlockSpec(w_grp.shape, const3, pipeline_mode=single),
            pl.BlockSpec((1, d_inner), const2, pipeline_mode=single),
            pl.BlockSpec(w_out.shape, const2, pipeline_mode=single),
        ],
        out_specs=pl.BlockSpec((1, tm, d_model), lambda b, t: (b, t, 0)),
        out_shape=jax.ShapeDtypeStruct(x.shape, x.dtype),
        scratch_shapes=[
            pltpu.VMEM((HALO, d_inner), F32),
            pltpu.VMEM((tm, d_inner), BF16),
        ],
        compiler_params=pltpu.CompilerParams(
            dimension_semantics=("arbitrary", "arbitrary"),
            vmem_limit_bytes=48 * 1024 * 1024,
        ),
        name="pool_layer",
    )(x, g, w_in, w_grp, scale, w_out)


LEVELS_ROBUST = (("pair", 2), ("half", 64), ("half", 32), ("half", 16), ("half", 8),
                 ("half", 4))
LEVELS_FAST = (("diag", 16), ("half", 64), ("half", 32))
DIAG_POS_STEPS = 8
FAST_MAX_LOG2 = 100.0


def _half_level_operand(q, k, b, n):
    lanes = b.shape[1]
    half = n // 2

    def ref_row(idx, rows):
        return jnp.broadcast_to(b[idx:idx + 1, :], (rows, lanes))

    if half >= V7X_SUBLANES:
        pieces = []
        for j in range(CHUNK // n):
            lo, mid, hi = j * n, j * n + half, (j + 1) * n
            ref = ref_row(mid - 1, half)
            pieces.append(k[lo:mid] * jnp.exp2(ref - b[lo:mid]))
            pieces.append(q[mid:hi] * jnp.exp2(b[mid:hi] - ref))
        return jnp.concatenate(pieces, axis=0)

    sub = lax.broadcasted_iota(jnp.int32, (V7X_SUBLANES, lanes), 0)
    if n == 8:
        ref = jnp.concatenate(
            [ref_row(r0 + 3, V7X_SUBLANES) for r0 in range(0, CHUNK, V7X_SUBLANES)], axis=0)
    elif n == 4:
        ref = jnp.concatenate(
            [jnp.where(sub < 4, ref_row(r0 + 1, V7X_SUBLANES), ref_row(r0 + 5, V7X_SUBLANES))
             for r0 in range(0, CHUNK, V7X_SUBLANES)], axis=0)
    else:
        raise NotImplementedError(n)
    row = lax.broadcasted_iota(jnp.int32, b.shape, 0)
    qk = jnp.where((row % n) >= half, q, k)
    return qk * jnp.exp2(-jnp.abs(b - ref))


def _diag_level_operands(q, k, b, n):
    lanes = b.shape[1]
    xq, xk = [], []
    for lo in range(0, CHUNK, n):
        mid = lo + n // 2
        ref = jnp.broadcast_to(b[mid - 1:mid, :], (n, lanes))
        wq = jnp.exp2(b[lo:lo + n] - ref)
        xq.append(q[lo:lo + n] * wq)
        xk.append(k[lo:lo + n] * (1.0 / wq))
    return jnp.concatenate(xq, axis=0), jnp.concatenate(xk, axis=0)


def _query_row_blocks(kind, n):
    if kind == "half" and n // 2 >= V7X_SUBLANES:
        return [(j * n + n // 2, n // 2) for j in range(CHUNK // n)]
    return [(0, CHUNK)]


def _level_mask(kind, n, r0):
    t = r0 + lax.broadcasted_iota(jnp.int32, (V7X_SUBLANES, CHUNK), 0)
    s = lax.broadcasted_iota(jnp.int32, (V7X_SUBLANES, CHUNK), 1)
    same = (t // n) == (s // n)
    if kind == "half":
        return same & ((t % n) >= n // 2) & ((s % n) < n // 2)
    return same & (t >= s)


def _hgrn_kernel(x_ref, g_ref, win_ref, lbp_ref, ng_ref, wout_ref, fg_ref, o_ref,
                 hn_ref, u_even, u_odd, st_even, st_odd, ob_ref, *,
                 tm, d_inner, layer, levels):
    t = pl.program_id(1)
    hps = HEADS_PER_STEP
    gw = hps * HEAD_DIM
    n_groups = d_inner // gw
    n_chunks = tm // CHUNK
    assert n_groups % 2 == 0

    @pl.when(t == 0)
    def _():
        st_even[...] = jnp.zeros_like(st_even)
        st_odd[...] = jnp.zeros_like(st_odd)

    x = x_ref[0]
    hn_ref[...] = _rms_norm(x, g_ref[...]).astype(BF16)

    row = lax.broadcasted_iota(jnp.int32, (CHUNK, CHUNK), 0)
    col = lax.broadcasted_iota(jnp.int32, (CHUNK, CHUNK), 1)
    tril = (row >= col).astype(BF16)
    assert _query_row_blocks(*levels[0]) == [(0, CHUNK)]
    masks = {(kind, n, r0 + p): _level_mask(kind, n, r0 + p)
             for kind, n in levels for r0, nr in _query_row_blocks(kind, n)
             for p in range(0, nr, V7X_SUBLANES)}
    odd =(lax.broadcasted_iota(jnp.int32, (CHUNK, gw), 0) % 2) == 1

    def project_piece(g, u_ref, sec):
        cols = slice(sec * gw, (sec + 1) * gw)
        u_ref[:, cols] = _dot(hn_ref[...], win_ref[g, :, cols])

    def group_consts(g):
        lbp = lbp_ref[:, g].astype(F32)
        e = jnp.exp(lbp - jnp.max(lbp, axis=0, keepdims=True))
        p = e / jnp.sum(e, axis=0, keepdims=True)
        lb = jnp.sum(p[:layer + 1], axis=0) - p[0]
        return lb, 1.0 - lb, ng_ref[g]

    def stage_gates(u_ref, c, lb, one_m_lb):
        rows = slice(c * CHUNK, (c + 1) * CHUNK)
        q = u_ref[rows, 0 * gw:1 * gw]
        fz = u_ref[rows, 1 * gw:2 * gw]
        ez = jnp.exp(-jnp.abs(fz))
        rz = 1.0 / (1.0 + ez)
        pos = fz >= 0
        sig = jnp.where(pos, rz, ez * rz)
        sig_neg = jnp.where(pos, ez * rz, rz)
        f = lb + one_m_lb * sig
        lf = jnp.log2(f)
        k = one_m_lb * sig_neg
        hi = lf.astype(BF16)
        r1 = lf - hi.astype(F32)
        mid = r1.astype(BF16)
        lo = (r1 - mid.astype(F32)).astype(BF16)
        cs = _dot(tril, jnp.concatenate([hi, mid, lo], axis=1))
        return q, k, f, cs

    def stage_intra(q, k, f, cs):
        b = cs[:, 0:gw] + cs[:, gw:2 * gw] + cs[:, 2 * gw:3 * gw]
        a = [[None] * (CHUNK // V7X_SUBLANES) for _ in range(hps)]
        for kind, n in levels:
            if kind == "half":
                xq = xk = _half_level_operand(q, k, b, n)
            elif kind == "diag":
                xq, xk = _diag_level_operands(q, k, b, n)
            else:
                xq = jnp.where(odd, q * f, q)
                xk = jnp.where(odd, k * (1.0 / f), k)
            blocks = _query_row_blocks(kind, n)
            lhs = jnp.concatenate([xq[r0:r0 + nr] for r0, nr in blocks], axis=0).astype(BF16)
            rhs = xk.astype(BF16)
            for j in range(hps):
                hl = slice(j * HEAD_DIM, (j + 1) * HEAD_DIM)
                prod = _dot_nt(lhs[:, hl], rhs[:, hl])
                off = 0
                for r0, nr in blocks:
                    for p in range(0, nr, V7X_SUBLANES):
                        piece = prod[off + p:off + p + V7X_SUBLANES]
                        idx = (r0 + p) // V7X_SUBLANES
                        old = 0.0 if a[j][idx] is None else a[j][idx]
                        a[j][idx] = jnp.where(masks[(kind, n, r0 + p)], piece, old)
                    off += nr
        qe = (q * jnp.exp2(b)).astype(BF16)
        b_last = b[CHUNK - 1:CHUNK, :]
        kd = (k * jnp.exp2(b_last - b)).astype(BF16)
        a = [jnp.concatenate(aj, axis=0).astype(BF16) for aj in a]
        return a, qe, kd, jnp.exp2(b_last)

    def stage_output(g, u_ref, c, ng, st, a, qe, kd, decay):
        rows = slice(c * CHUNK, (c + 1) * CHUNK)
        vb = u_ref[rows, 2 * gw:3 * gw].astype(BF16)
        gate = u_ref[rows, 3 * gw:4 * gw]
        for j in range(hps):
            hl = slice(j * HEAD_DIM, (j + 1) * HEAD_DIM)
            o = _dot(a[j], vb[:, hl]) + _dot_nt(qe[:, hl], st[j].astype(BF16))
            st[j] = st[j] * decay[:, hl] + _dot_tn(vb[:, hl], kd[:, hl])
            ms = jnp.mean(o * o, axis=-1, keepdims=True)
            on = o * lax.rsqrt(ms + EPS) * ng[:, hl]
            gt = gate[:, hl]
            ob_ref[g, rows, hl] = (on * (gt * _sigmoid(gt))).astype(BF16)

    def run_groups(groups, pieces):
        items = [(gi, c) for gi in range(len(groups)) for c in range(n_chunks)]
        consts, states = {}, {}
        gates_out, intra_out = {}, {}
        for step in range(len(items) + 2):
            if step < len(items):
                gi, c = items[step]
                g, pair, u_ref, st_ref = groups[gi]
                if c == 0:
                    consts[gi] = group_consts(g)
                gates_out[(gi, c)] = stage_gates(u_ref, c, consts[gi][0], consts[gi][1])
            if step >= 2:
                gi, c = items[step - 2]
                g, pair, u_ref, st_ref = groups[gi]
                if c == 0:
                    states[gi] = [st_ref[pair * hps + j] for j in range(hps)]
                stage_output(g, u_ref, c, consts[gi][2], states[gi], *intra_out.pop((gi, c)))
                if c == n_chunks - 1:
                    for j in range(hps):
                        st_ref[pair * hps + j] = states[gi][j]
            if step < len(pieces) and pieces[step] is not None:
                project_piece(*pieces[step])
            if 1 <= step <= len(items):
                intra_out[items[step - 1]] = stage_intra(*gates_out.pop(items[step - 1]))

    gb = GROUPS_PER_BLOCK
    assert gb % 2 == 0 and n_groups % gb == 0
    u_bufs, st_bufs = (u_even, u_odd), (st_even, st_odd)
    for sec in range(4):
        project_piece(0, u_even, sec)

    def block(g0, pair0, last):
        groups = [(g0 + gi, pair0 + gi // 2, u_bufs[gi % 2], st_bufs[gi % 2])
                  for gi in range(gb)]
        n_proj = gb - 1 if last else gb
        pieces = [(g0 + gi + 1, u_bufs[(gi + 1) % 2], sec)
                  for gi in range(n_proj) for sec in range(4)]
        run_groups(groups, pieces)

    def block_body(i, carry):
        block(i * gb, i * (gb // 2), False)
        return carry

    n_blocks = n_groups // gb
    if n_blocks > 1:
        lax.fori_loop(0, n_blocks - 1, block_body, 0)
    block((n_blocks - 1) * gb, (n_blocks - 1) * (gb // 2), True)

    og = jnp.concatenate([ob_ref[g] for g in range(n_groups)], axis=1)
    h2 = x + _dot(og, wout_ref[...])
    o_ref[0] = _rms_norm(h2, fg_ref[...])


def _hgrn_layer(x, g, w_in, lb_params, norm_g, w_out, final_g, *, tm, layer, levels):
    bsz, seq, d_model = x.shape
    d_inner = w_out.shape[0]
    n_heads = d_inner // HEAD_DIM
    gw = HEADS_PER_STEP * HEAD_DIM
    n_groups = d_inner // gw
    depth = lb_params.shape[0]
    w_grouped = (w_in.reshape(d_model, 4, n_groups, gw).transpose(2, 0, 1, 3)
                 .reshape(n_groups, d_model, 4 * gw))
    const2 = lambda b, t: (0, 0)
    const3 = lambda b, t: (0, 0, 0)
    const4 = lambda b, t: (0, 0, 0, 0)
    single = pl.Buffered(1)
    kern = functools.partial(_hgrn_kernel, tm=tm, d_inner=d_inner, layer=layer,
                             levels=levels)
    return pl.pallas_call(
        kern,
        grid=(bsz, seq // tm),
        in_specs=[
            pl.BlockSpec((1, tm, d_model), lambda b, t: (b, t, 0)),
            pl.BlockSpec((1, d_model), const2, pipeline_mode=single),
            pl.BlockSpec(w_grouped.shape, const3, pipeline_mode=single),
            pl.BlockSpec((depth, n_groups, 1, gw), const4, pipeline_mode=single),
            pl.BlockSpec((n_groups, 1, gw), const3, pipeline_mode=single),
            pl.BlockSpec(w_out.shape, const2, pipeline_mode=single),
            pl.BlockSpec((1, d_model), const2, pipeline_mode=single),
        ],
        out_specs=pl.BlockSpec((1, tm, d_model), lambda b, t: (b, t, 0)),
        out_shape=jax.ShapeDtypeStruct(x.shape, x.dtype),
        scratch_shapes=[
            pltpu.VMEM((tm, d_model), BF16),
            pltpu.VMEM((tm, 4 * gw), F32),
            pltpu.VMEM((tm, 4 * gw), F32),
            pltpu.VMEM((n_heads // 2, HEAD_DIM, HEAD_DIM), F32),
            pltpu.VMEM((n_heads // 2, HEAD_DIM, HEAD_DIM), F32),
            pltpu.VMEM((n_groups, tm, gw), BF16),
        ],
        compiler_params=pltpu.CompilerParams(
            dimension_semantics=("arbitrary", "arbitrary"),
            vmem_limit_bytes=56 * 1024 * 1024,
        ),
        name=f"hgrn_layer_{len(levels)}_levels",
    )(x, g, w_grouped, lb_params.reshape(depth, n_groups, 1, gw),
      norm_g.reshape(n_groups, 1, gw), w_out, final_g)


def kernel(x, norm_g, pool_w_in, pool_w_grp, pool_scale, pool_w_out, hgrn_w_in,
           hgrn_lower_bounds, hgrn_norm_g, hgrn_w_out, final_g):
    depth = norm_g.shape[0]
    assert depth == 2 and pool_w_in.shape[0] == 1 and hgrn_w_in.shape[0] == 1
    h = _pool_layer(
        x, norm_g[0:1], pool_w_in[0].astype(BF16), pool_w_grp[0].astype(BF16),
        pool_scale[0:1], pool_w_out[0].astype(BF16), tm=256)
    layer = 1
    p = jax.nn.softmax(hgrn_lower_bounds.astype(F32), axis=0)
    lb = (jnp.cumsum(p, axis=0) - p[0:1])[layer]
    fast_ok = jnp.all(DIAG_POS_STEPS * jnp.abs(jnp.log2(lb)) <= FAST_MAX_LOG2)
    operands = (h, norm_g[1:2], hgrn_w_in[0].astype(BF16), hgrn_lower_bounds,
                hgrn_norm_g[0], hgrn_w_out[0].astype(BF16), final_g.reshape(1, -1))
    return lax.cond(
        fast_ok,
        functools.partial(_hgrn_layer, tm=256, layer=layer, levels=LEVELS_FAST),
        functools.partial(_hgrn_layer, tm=256, layer=layer, levels=LEVELS_ROBUST),
        *operands)
```

```python
import functools

import jax
import jax.numpy as jnp
from jax import lax
from jax.experimental import pallas as pl
from jax.experimental.pallas import tpu as pltpu

EPS = 1e-6
CHUNK = 64
HEAD_DIM = 128
HEADS_PER_STEP = 2
POOL_TM = 256
HGRN_TM = 256
POOL_WINDOWS = (2, 4, 8, 16)
HALO = 16
N_MIXERS = 2

V7X_VMEM_BYTES = 64 * 1024 * 1024
V7X_LANES = 128
V7X_SUBLANES = 8

F32 = jnp.float32
BF16 = jnp.bfloat16


def _rms_norm(x, g):
    ms = jnp.mean(x * x, axis=-1, keepdims=True)
    return x * lax.rsqrt(ms + EPS) * g


def _sigmoid(x):
    return 1.0 / (1.0 + jnp.exp(-x))


def _dot(a, b):
    return jnp.dot(a, b, preferred_element_type=F32)


def _dot_nt(a, b):
    return lax.dot_general(a, b, (((1,), (1,)), ((), ())), preferred_element_type=F32)


def _dot_tn(a, b):
    return lax.dot_general(a, b, (((0,), (0,)), ((), ())), preferred_element_type=F32)


def _pool_kernel(x_ref, g_ref, win_ref, wgrp_ref, scale_ref, wout_ref, o_ref,
                 carry_ref, y_ref, *, tm, d_inner):
    t = pl.program_id(1)
    gsz = d_inner // len(POOL_WINDOWS)

    @pl.when(t == 0)
    def _():
        carry_ref[...] = jnp.zeros_like(carry_ref)

    x = x_ref[0]
    hn = _rms_norm(x, g_ref[...]).astype(BF16)
    pos = t * tm + lax.broadcasted_iota(jnp.int32, (tm, 1), 0)

    def project(g):
        v = _dot(hn, win_ref[:, g * gsz:(g + 1) * gsz])
        gate = _dot(hn, win_ref[:, d_inner + g * gsz:d_inner + (g + 1) * gsz])
        return v, gate

    projected = project(0)
    for g, w in enumerate(POOL_WINDOWS):
        cols = slice(g * gsz, (g + 1) * gsz)
        v, gate = projected
        if g + 1 < len(POOL_WINDOWS):
            projected = project(g + 1)
        s = jnp.concatenate([carry_ref[:, cols], v], axis=0)
        carry_ref[:, cols] = v[tm - HALO:, :]
        shift = 1
        while shift < w:
            s = s + pltpu.roll(s, shift, axis=0)
            shift *= 2
        inv_cnt = 1.0 / jnp.minimum(pos + 1, w).astype(F32)
        p = s[HALO:, :] * inv_cnt - v
        m = _dot(p.astype(BF16), wgrp_ref[g]) * scale_ref[:, cols]
        y_ref[:, cols] = (m * (gate * _sigmoid(gate))).astype(BF16)

    o_ref[0] = x + _dot(y_ref[...], wout_ref[...])


def _pool_layer(x, g, w_in, w_grp, scale, w_out, *, tm):
    bsz, seq, d_model = x.shape
    d_inner = w_out.shape[0]
    const2 = lambda b, t: (0, 0)
    const3 = lambda b, t: (0, 0, 0)
    single = pl.Buffered(1)
    kern = functools.partial(_pool_kernel, tm=tm, d_inner=d_inner)
    return pl.pallas_call(
        kern,
        grid=(bsz, seq // tm),
        in_specs=[
            pl.BlockSpec((1, tm, d_model), lambda b, t: (b, t, 0)),
            pl.BlockSpec((1, d_model), const2, pipeline_mode=single),
            pl.BlockSpec(w_in.shape, const2, pipeline_mode=single),
            pl.BlockSpec(w_grp.shape, const3, pipeline_mode=single),
            pl.BlockSpec((1, d_inner), const2, pipeline_mode=single),
            pl.BlockSpec(w_out.shape, const2, pipeline_mode=single),
        ],
        out_specs=pl.BlockSpec((1, tm, d_model), lambda b, t: (b, t, 0)),
        out_shape=jax.ShapeDtypeStruct(x.shape, x.dtype),
        scratch_shapes=[
            pltpu.VMEM((HALO, d_inner), F32),
            pltpu.VMEM((tm, d_inner), BF16),
        ],
        compiler_params=pltpu.CompilerParams(
            dimension_semantics=("arbitrary", "arbitrary"),
            vmem_limit_bytes=48 * 1024 * 1024,
        ),
        name="pool_layer",
    )(x, g, w_in, w_grp, scale, w_out)


LEVELS_ROBUST = (("pair", 2), ("half", 64), ("half", 32), ("half", 16), ("half", 8),
                 ("half", 4))
LEVELS_FAST = (("diag", 16), ("half", 64), ("half", 32))
DIAG_POS_STEPS = 8
FAST_MAX_LOG2 = 100.0


def _half_level_operand(q, k, b, n):
    lanes = b.shape[1]
    half = n // 2

    def ref_row(idx, rows):
        return jnp.broadcast_to(b[idx:idx + 1, :], (rows, lanes))

    if half >= V7X_SUBLANES:
        pieces = []
        for j in range(CHUNK // n):
            lo, mid, hi = j * n, j * n + half, (j + 1) * n
            ref = ref_row(mid - 1, half)
            pieces.append(k[lo:mid] * jnp.exp2(ref - b[lo:mid]))
            pieces.append(q[mid:hi] * jnp.exp2(b[mid:hi] - ref))
        return jnp.concatenate(pieces, axis=0)

    sub = lax.broadcasted_iota(jnp.int32, (V7X_SUBLANES, lanes), 0)
    if n == 8:
        ref = jnp.concatenate(
            [ref_row(r0 + 3, V7X_SUBLANES) for r0 in range(0, CHUNK, V7X_SUBLANES)], axis=0)
    elif n == 4:
        ref = jnp.concatenate(
            [jnp.where(sub < 4, ref_row(r0 + 1, V7X_SUBLANES), ref_row(r0 + 5, V7X_SUBLANES))
             for r0 in range(0, CHUNK, V7X_SUBLANES)], axis=0)
    else:
        raise NotImplementedError(n)
    row = lax.broadcasted_iota(jnp.int32, b.shape, 0)
    qk = jnp.where((row % n) >= half, q, k)
    return qk * jnp.exp2(-jnp.abs(b - ref))


def _diag_level_operands(q, k, b, n):
    lanes = b.shape[1]
    xq, xk = [], []
    for lo in range(0, CHUNK, n):
        mid = lo + n // 2
        ref = jnp.broadcast_to(b[mid - 1:mid, :], (n, lanes))
        wq = jnp.exp2(b[lo:lo + n] - ref)
        xq.append(q[lo:lo + n] * wq)
        xk.append(k[lo:lo + n] * (1.0 / wq))
    return jnp.concatenate(xq, axis=0), jnp.concatenate(xk, axis=0)


def _query_row_blocks(kind, n):
    if kind == "half" and n // 2 >= V7X_SUBLANES:
        return [(j * n + n // 2, n // 2) for j in range(CHUNK // n)]
    return [(0, CHUNK)]


def _level_mask(kind, n, r0):
    t = r0 + lax.broadcasted_iota(jnp.int32, (V7X_SUBLANES, CHUNK), 0)
    s = lax.broadcasted_iota(jnp.int32, (V7X_SUBLANES, CHUNK), 1)
    same = (t // n) == (s // n)
    if kind == "half":
        return same & ((t % n) >= n // 2) & ((s % n) < n // 2)
    return same & (t >= s)


def _hgrn_kernel(x_ref, g_ref, win_ref, lbp_ref, ng_ref, wout_ref, fg_ref, o_ref,
                 hn_ref, u_even, u_odd, st_ref, ob_ref, yacc_ref, *,
                 tm, d_inner, layer, levels):
    t = pl.program_id(1)
    hps = HEADS_PER_STEP
    gw = hps * HEAD_DIM
    n_groups = d_inner // gw
    n_chunks = tm // CHUNK
    d_model = x_ref.shape[-1]

    @pl.when(t == 0)
    def _():
        st_ref[...] = jnp.zeros_like(st_ref)

    x = x_ref[0]
    hn_ref[...] = _rms_norm(x, g_ref[...]).astype(BF16)

    row = lax.broadcasted_iota(jnp.int32, (CHUNK, CHUNK), 0)
    col = lax.broadcasted_iota(jnp.int32, (CHUNK, CHUNK), 1)
    tril = (row >= col).astype(BF16)
    assert _query_row_blocks(*levels[0]) == [(0, CHUNK)]
    masks = {(kind, n, r0 + p): _level_mask(kind, n, r0 + p)
             for kind, n in levels for r0, nr in _query_row_blocks(kind, n)
             for p in range(0, nr, V7X_SUBLANES)}
    odd = (lax.broadcasted_iota(jnp.int32, (CHUNK, gw), 0) % 2) == 1
    u_bufs = (u_even, u_odd)

    def project_piece(g, sec):
        c0 = sec * d_inner + g * gw
        u_bufs[g % 2][:, sec * gw:(sec + 1) * gw] = _dot(hn_ref[...], win_ref[:, c0:c0 + gw])

    pair_k = 2 * gw
    half_cols = d_model // 2

    def out_piece(pair, half):
        cols = slice(half * half_cols, (half + 1) * half_cols)
        part = _dot(ob_ref[:, pair * pair_k:(pair + 1) * pair_k],
                    wout_ref[pair * pair_k:(pair + 1) * pair_k, cols])
        if pair == 0:
            yacc_ref[:, cols] = part
        else:
            yacc_ref[:, cols] += part

    def group_consts(g):
        lbp = lbp_ref[:, g].astype(F32)
        e = jnp.exp(lbp - jnp.max(lbp, axis=0, keepdims=True))
        p = e / jnp.sum(e, axis=0, keepdims=True)
        lb = jnp.sum(p[:layer + 1], axis=0) - p[0]
        return lb, 1.0 - lb, ng_ref[g]

    def stage_gates(u_ref, c, lb, one_m_lb):
        rows = slice(c * CHUNK, (c + 1) * CHUNK)
        q = u_ref[rows, 0 * gw:1 * gw]
        fz = u_ref[rows, 1 * gw:2 * gw]
        ez = jnp.exp(-jnp.abs(fz))
        rz = 1.0 / (1.0 + ez)
        pos = fz >= 0
        sig = jnp.where(pos, rz, ez * rz)
        sig_neg = jnp.where(pos, ez * rz, rz)
        f = lb + one_m_lb * sig
        lf = jnp.log2(f)
        k = one_m_lb * sig_neg
        hi = lf.astype(BF16)
        r1 = lf - hi.astype(F32)
        mid = r1.astype(BF16)
        lo = (r1 - mid.astype(F32)).astype(BF16)
        cs = _dot(tril, jnp.concatenate([hi, mid, lo], axis=1))
        return q, k, f, cs

    def stage_intra(q, k, f, cs):
        b = cs[:, 0:gw] + cs[:, gw:2 * gw] + cs[:, 2 * gw:3 * gw]
        a = [[None] * (CHUNK // V7X_SUBLANES) for _ in range(hps)]
        for kind, n in levels:
            if kind == "half":
                xq = xk = _half_level_operand(q, k, b, n)
            elif kind == "diag":
                xq, xk = _diag_level_operands(q, k, b, n)
            else:
                xq = jnp.where(odd, q * f, q)
                xk = jnp.where(odd, k * (1.0 / f), k)
            blocks = _query_row_blocks(kind, n)
            lhs = jnp.concatenate([xq[r0:r0 + nr] for r0, nr in blocks], axis=0).astype(BF16)
            rhs = xk.astype(BF16)
            for j in range(hps):
                hl = slice(j * HEAD_DIM, (j + 1) * HEAD_DIM)
                prod = _dot_nt(lhs[:, hl], rhs[:, hl])
                off = 0
                for r0, nr in blocks:
                    for p in range(0, nr, V7X_SUBLANES):
                        piece = prod[off + p:off + p + V7X_SUBLANES]
                        idx = (r0 + p) // V7X_SUBLANES
                        old = 0.0 if a[j][idx] is None else a[j][idx]
                        a[j][idx] = jnp.where(masks[(kind, n, r0 + p)], piece, old)
                    off += nr
        qe = (q * jnp.exp2(b)).astype(BF16)
        b_last = b[CHUNK - 1:CHUNK, :]
        kd = (k * jnp.exp2(b_last - b)).astype(BF16)
        a = [jnp.concatenate(aj, axis=0).astype(BF16) for aj in a]
        return a, qe, kd, jnp.exp2(b_last)

    def stage_output(g, c, ng, st, a, qe, kd, decay):
        rows = slice(c * CHUNK, (c + 1) * CHUNK)
        u_ref = u_bufs[g % 2]
        vb = u_ref[rows, 2 * gw:3 * gw].astype(BF16)
        gate = u_ref[rows, 3 * gw:4 * gw]
        for j in range(hps):
            hl = slice(j * HEAD_DIM, (j + 1) * HEAD_DIM)
            o = _dot(a[j], vb[:, hl]) + _dot_nt(qe[:, hl], st[j].astype(BF16))
            st[j] = st[j] * decay[:, hl] + _dot_tn(vb[:, hl], kd[:, hl])
            ms = jnp.mean(o * o, axis=-1, keepdims=True)
            on = o * lax.rsqrt(ms + EPS) * ng[:, hl]
            gt = gate[:, hl]
            ob_ref[rows, g * gw + j * HEAD_DIM:g * gw + (j + 1) * HEAD_DIM] = (
                on * (gt * _sigmoid(gt))).astype(BF16)

    items = [(g, c) for g in range(n_groups) for c in range(n_chunks)]
    pieces = {step: [] for step in range(len(items) + 2)}
    for g in range(1, n_groups):
        for sec in range(4):
            pieces[(g - 1) * n_chunks + sec * n_chunks // 4].append(
                functools.partial(project_piece, g, sec))
    for pair in range(n_groups // 2 - 1):
        done = (2 * pair + 2) * n_chunks + 1
        for half in range(2):
            pieces[done + 1 + half].append(functools.partial(out_piece, pair, half))

    for sec in range(4):
        project_piece(0, sec)
    consts, states, gates_out, intra_out = {}, {}, {}, {}
    for step in range(len(items) + 2):
        if step < len(items):
            g, c = items[step]
            if c == 0:
                consts[g] = group_consts(g)
            gates_out[(g, c)] = stage_gates(u_bufs[g % 2], c, consts[g][0], consts[g][1])
        if step >= 2:
            g, c = items[step - 2]
            if c == 0:
                states[g] = [st_ref[g * hps + j] for j in range(hps)]
            stage_output(g, c, consts[g][2], states[g], *intra_out.pop((g, c)))
            if c == n_chunks - 1:
                for j in range(hps):
                    st_ref[g * hps + j] = states[g][j]
        for piece in pieces[step]:
            piece()
        if 1 <= step <= len(items):
            intra_out[items[step - 1]] = stage_intra(*gates_out.pop(items[step - 1]))

    k_done = (n_groups // 2 - 1) * pair_k
    y = yacc_ref[...] + _dot(ob_ref[:, k_done:], wout_ref[k_done:, :])
    o_ref[0] = _rms_norm(x + y, fg_ref[...])


def _hgrn_layer(x, g, w_in, lb_params, norm_g, w_out, final_g, *, tm, layer, levels):
    bsz, seq, d_model = x.shape
    d_inner = w_out.shape[0]
    n_heads = d_inner // HEAD_DIM
    gw = HEADS_PER_STEP * HEAD_DIM
    n_groups = d_inner // gw
    depth = lb_params.shape[0]
    const2 = lambda b, t: (0, 0)
    const3 = lambda b, t: (0, 0, 0)
    const4 = lambda b, t: (0, 0, 0, 0)
    single = pl.Buffered(1)
    kern = functools.partial(_hgrn_kernel, tm=tm, d_inner=d_inner, layer=layer,
                             levels=levels)
    return pl.pallas_call(
        kern,
        grid=(bsz, seq // tm),
        in_specs=[
            pl.BlockSpec((1, tm, d_model), lambda b, t: (b, t, 0)),
            pl.BlockSpec((1, d_model), const2, pipeline_mode=single),
            pl.BlockSpec(w_in.shape, const2, pipeline_mode=single),
            pl.BlockSpec((depth, n_groups, 1, gw), const4, pipeline_mode=single),
            pl.BlockSpec((n_groups, 1, gw), const3, pipeline_mode=single),
            pl.BlockSpec(w_out.shape, const2, pipeline_mode=single),
            pl.BlockSpec((1, d_model), const2, pipeline_mode=single),
        ],
        out_specs=pl.BlockSpec((1, tm, d_model), lambda b, t: (b, t, 0)),
        out_shape=jax.ShapeDtypeStruct(x.shape, x.dtype),
        scratch_shapes=[
            pltpu.VMEM((tm, d_model), BF16),
            pltpu.VMEM((tm, 4 * gw), F32),
            pltpu.VMEM((tm, 4 * gw), F32),
            pltpu.VMEM((n_heads, HEAD_DIM, HEAD_DIM), F32),
            pltpu.VMEM((tm, d_inner), BF16),
            pltpu.VMEM((tm, d_model), F32),
        ],
        compiler_params=pltpu.CompilerParams(
            dimension_semantics=("arbitrary", "arbitrary"),
            vmem_limit_bytes=56 * 1024 * 1024,
        ),
        name=f"hgrn_layer_{len(levels)}_levels",
    )(x, g, w_in, lb_params.reshape(depth, n_groups, 1, gw),
      norm_g.reshape(n_groups, 1, gw), w_out, final_g)


def kernel(x, norm_g, pool_w_in, pool_w_grp, pool_scale, pool_w_out, hgrn_w_in,
           hgrn_lower_bounds, hgrn_norm_g, hgrn_w_out, final_g):
    depth = norm_g.shape[0]
    assert depth == 2 and pool_w_in.shape[0] == 1 and hgrn_w_in.shape[0] == 1
    h = _pool_layer(
        x, norm_g[0:1], pool_w_in[0].astype(BF16), pool_w_grp[0].astype(BF16),
        pool_scale[0:1], pool_w_out[0].astype(BF16), tm=POOL_TM)
    layer = 1
    p = jax.nn.softmax(hgrn_lower_bounds.astype(F32), axis=0)
    lb = (jnp.cumsum(p, axis=0) - p[0:1])[layer]
    fast_ok = jnp.all(DIAG_POS_STEPS * jnp.abs(jnp.log2(lb)) <= FAST_MAX_LOG2)
    operands = (h, norm_g[1:2], hgrn_w_in[0].astype(BF16), hgrn_lower_bounds,
                hgrn_norm_g[0], hgrn_w_out[0].astype(BF16), final_g.reshape(1, -1))
    return lax.cond(
        fast_ok,
        functools.partial(_hgrn_layer, tm=HGRN_TM, layer=layer, levels=LEVELS_FAST),
        functools.partial(_hgrn_layer, tm=HGRN_TM, layer=layer, levels=LEVELS_ROBUST),
        *operands)
```

```python
import functools

import jax
import jax.numpy as jnp
from jax import lax
from jax.experimental import pallas as pl
from jax.experimental.pallas import tpu as pltpu

EPS = 1e-6
CHUNK = 64
HEAD_DIM = 128
HEADS_PER_STEP = 2
POOL_TM = 256
HGRN_TM = 256
POOL_WINDOWS = (2, 4, 8, 16)
CUMSUM_TERMS = 3
HALO = 16
N_MIXERS = 2

V7X_VMEM_BYTES = 64 * 1024 * 1024
V7X_LANES = 128
V7X_SUBLANES = 8

F32 = jnp.float32
BF16 = jnp.bfloat16


def _rms_norm(x, g):
    ms = jnp.mean(x * x, axis=-1, keepdims=True)
    return x * lax.rsqrt(ms + EPS) * g


def _sigmoid(x):
    return 1.0 / (1.0 + jnp.exp(-x))


def _dot(a, b):
    return jnp.dot(a, b, preferred_element_type=F32)


def _dot_nt(a, b):
    return lax.dot_general(a, b, (((1,), (1,)), ((), ())), preferred_element_type=F32)


def _dot_tn(a, b):
    return lax.dot_general(a, b, (((0,), (0,)), ((), ())), preferred_element_type=F32)


def _pool_kernel(x_ref, g_ref, win_ref, wgrp_ref, scale_ref, wout_ref, o_ref,
                 carry_ref, y_ref, *, tm, d_inner):
    t = pl.program_id(1)
    gsz = d_inner // len(POOL_WINDOWS)

    @pl.when(t == 0)
    def _():
        carry_ref[...] = jnp.zeros_like(carry_ref)

    x = x_ref[0]
    hn = _rms_norm(x, g_ref[...]).astype(BF16)
    pos = t * tm + lax.broadcasted_iota(jnp.int32, (tm, 1), 0)

    def project(g):
        v = _dot(hn, win_ref[:, g * gsz:(g + 1) * gsz])
        gate = _dot(hn, win_ref[:, d_inner + g * gsz:d_inner + (g + 1) * gsz])
        return v, gate

    projected = project(0)
    for g, w in enumerate(POOL_WINDOWS):
        cols = slice(g * gsz, (g + 1) * gsz)
        v, gate = projected
        if g + 1 < len(POOL_WINDOWS):
            projected = project(g + 1)
        s = jnp.concatenate([carry_ref[:, cols], v], axis=0)
        carry_ref[:, cols] = v[tm - HALO:, :]
        shift = 1
        while shift < w:
            s = s + pltpu.roll(s, shift, axis=0)
            shift *= 2
        inv_cnt = 1.0 / jnp.minimum(pos + 1, w).astype(F32)
        p = s[HALO:, :] * inv_cnt - v
        m = _dot(p.astype(BF16), wgrp_ref[g]) * scale_ref[:, cols]
        y_ref[:, cols] = (m * (gate * _sigmoid(gate))).astype(BF16)

    o_ref[0] = x + _dot(y_ref[...], wout_ref[...])


def _pool_layer(x, g, w_in, w_grp, scale, w_out, *, tm):
    bsz, seq, d_model = x.shape
    d_inner = w_out.shape[0]
    const2 = lambda b, t: (0, 0)
    const3 = lambda b, t: (0, 0, 0)
    single = pl.Buffered(1)
    kern = functools.partial(_pool_kernel, tm=tm, d_inner=d_inner)
    return pl.pallas_call(
        kern,
        grid=(bsz, seq // tm),
        in_specs=[
            pl.BlockSpec((1, tm, d_model), lambda b, t: (b, t, 0)),
            pl.BlockSpec((1, d_model), const2, pipeline_mode=single),
            pl.BlockSpec(w_in.shape, const2, pipeline_mode=single),
            pl.BlockSpec(w_grp.shape, const3, pipeline_mode=single),
            pl.BlockSpec((1, d_inner), const2, pipeline_mode=single),
            pl.BlockSpec(w_out.shape, const2, pipeline_mode=single),
        ],
        out_specs=pl.BlockSpec((1, tm, d_model), lambda b, t: (b, t, 0)),
        out_shape=jax.ShapeDtypeStruct(x.shape, x.dtype),
        scratch_shapes=[
            pltpu.VMEM((HALO, d_inner), F32),
            pltpu.VMEM((tm, d_inner), BF16),
        ],
        compiler_params=pltpu.CompilerParams(
            dimension_semantics=("arbitrary", "arbitrary"),
            vmem_limit_bytes=48 * 1024 * 1024,
        ),
        name="pool_layer",
    )(x, g, w_in, w_grp, scale, w_out)


LEVELS_ROBUST = (("pair", 2), ("half", 64), ("half", 32), ("half", 16), ("half", 8),
                 ("half", 4))
LEVELS_FAST = (("diag", 16), ("half", 64), ("half", 32))
DIAG_POS_STEPS = 8
FAST_MAX_LOG2 = 100.0


def _half_level_operand(q, k, b, n):
    lanes = b.shape[1]
    half = n // 2

    def ref_row(idx, rows):
        return jnp.broadcast_to(b[idx:idx + 1, :], (rows, lanes))

    if half >= V7X_SUBLANES:
        pieces = []
        for j in range(CHUNK // n):
            lo, mid, hi = j * n, j * n + half, (j + 1) * n
            ref = ref_row(mid - 1, half)
            pieces.append(k[lo:mid] * jnp.exp2(ref - b[lo:mid]))
            pieces.append(q[mid:hi] * jnp.exp2(b[mid:hi] - ref))
        return jnp.concatenate(pieces, axis=0)

    sub = lax.broadcasted_iota(jnp.int32, (V7X_SUBLANES, lanes), 0)
    if n == 8:
        ref = jnp.concatenate(
            [ref_row(r0 + 3, V7X_SUBLANES) for r0 in range(0, CHUNK, V7X_SUBLANES)], axis=0)
    elif n == 4:
        ref = jnp.concatenate(
            [jnp.where(sub < 4, ref_row(r0 + 1, V7X_SUBLANES), ref_row(r0 + 5, V7X_SUBLANES))
             for r0 in range(0, CHUNK, V7X_SUBLANES)], axis=0)
    else:
        raise NotImplementedError(n)
    row = lax.broadcasted_iota(jnp.int32, b.shape, 0)
    qk = jnp.where((row % n) >= half, q, k)
    return qk * jnp.exp2(-jnp.abs(b - ref))


def _diag_level_operands(q, k, b, n):
    lanes = b.shape[1]
    xq, xk = [], []
    for lo in range(0, CHUNK, n):
        mid = lo + n // 2
        ref = jnp.broadcast_to(b[mid - 1:mid, :], (n, lanes))
        wq = jnp.exp2(b[lo:lo + n] - ref)
        xq.append(q[lo:lo + n] * wq)
        xk.append(k[lo:lo + n] * (1.0 / wq))
    return jnp.concatenate(xq, axis=0), jnp.concatenate(xk, axis=0)


def _query_row_blocks(kind, n):
    if kind == "half" and n // 2 >= V7X_SUBLANES:
        return [(j * n + n // 2, n // 2) for j in range(CHUNK // n)]
    return [(0, CHUNK)]


def _level_mask(kind, n, r0):
    t = r0 + lax.broadcasted_iota(jnp.int32, (V7X_SUBLANES, CHUNK), 0)
    s = lax.broadcasted_iota(jnp.int32, (V7X_SUBLANES, CHUNK), 1)
    same = (t // n) == (s // n)
    if kind == "half":
        return same & ((t % n) >= n // 2) & ((s % n) < n // 2)
    return same & (t >= s)


def _hgrn_kernel(x_ref, g_ref, win_ref, lbp_ref, ng_ref, wout_ref, fg_ref, o_ref,
                 hn_ref, u_even, u_odd, st_ref, ob_ref, yacc_ref, *,
                 tm, d_inner, layer, levels):
    t = pl.program_id(1)
    hps = HEADS_PER_STEP
    gw = hps * HEAD_DIM
    n_groups = d_inner // gw
    n_chunks = tm // CHUNK
    d_model = x_ref.shape[-1]

    @pl.when(t == 0)
    def _():
        st_ref[...] = jnp.zeros_like(st_ref)

    x = x_ref[0]
    hn_ref[...] = _rms_norm(x, g_ref[...]).astype(BF16)

    row = lax.broadcasted_iota(jnp.int32, (CHUNK, CHUNK), 0)
    col = lax.broadcasted_iota(jnp.int32, (CHUNK, CHUNK), 1)
    tril = (row >= col).astype(BF16)
    assert _query_row_blocks(*levels[0]) == [(0, CHUNK)]
    masks = {(kind, n, r0 + p): _level_mask(kind, n, r0 + p)
             for kind, n in levels for r0, nr in _query_row_blocks(kind, n)
             for p in range(0, nr, V7X_SUBLANES)}
    odd = (lax.broadcasted_iota(jnp.int32, (CHUNK, gw), 0) % 2) == 1
    u_bufs = (u_even, u_odd)

    def project_piece(g, sec):
        c0 = sec * d_inner + g * gw
        u_bufs[g % 2][:, sec * gw:(sec + 1) * gw] = _dot(hn_ref[...], win_ref[:, c0:c0 + gw])

    pair_k = 2 * gw
    half_cols = d_model // 2

    def out_piece(pair, half):
        cols = slice(half * half_cols, (half + 1) * half_cols)
        part = _dot(ob_ref[:, pair * pair_k:(pair + 1) * pair_k],
                    wout_ref[pair * pair_k:(pair + 1) * pair_k, cols])
        if pair == 0:
            yacc_ref[:, cols] = part
        else:
            yacc_ref[:, cols] += part

    def group_consts(g):
        lbp = lbp_ref[:, g].astype(F32)
        e = jnp.exp(lbp - jnp.max(lbp, axis=0, keepdims=True))
        p = e / jnp.sum(e, axis=0, keepdims=True)
        lb = jnp.sum(p[:layer + 1], axis=0) - p[0]
        return lb, 1.0 - lb, ng_ref[g]

    def stage_gates(u_ref, c, lb, one_m_lb):
        rows = slice(c * CHUNK, (c + 1) * CHUNK)
        q = u_ref[rows, 0 * gw:1 * gw]
        fz = u_ref[rows, 1 * gw:2 * gw]
        ez = jnp.exp(-jnp.abs(fz))
        rz = 1.0 / (1.0 + ez)
        pos = fz >= 0
        sig = jnp.where(pos, rz, ez * rz)
        sig_neg = jnp.where(pos, ez * rz, rz)
        f = lb + one_m_lb * sig
        lf = jnp.log2(f)
        k = one_m_lb * sig_neg
        parts, rest = [], lf
        for i in range(CUMSUM_TERMS):
            parts.append(rest.astype(BF16))
            if i + 1 < CUMSUM_TERMS:
                rest = rest - parts[-1].astype(F32)
        cs = _dot(tril, jnp.concatenate(parts, axis=1))
        return q, k, f, cs

    def stage_intra(q, k, f, cs):
        b = sum(cs[:, i * gw:(i + 1) * gw] for i in range(1, CUMSUM_TERMS)) + cs[:, 0:gw]
        a = [[None] * (CHUNK // V7X_SUBLANES) for _ in range(hps)]
        for kind, n in levels:
            if kind == "half":
                xq = xk = _half_level_operand(q, k, b, n)
            elif kind == "diag":
                xq, xk = _diag_level_operands(q, k, b, n)
            else:
                xq = jnp.where(odd, q * f, q)
                xk = jnp.where(odd, k * (1.0 / f), k)
            blocks = _query_row_blocks(kind, n)
            lhs = jnp.concatenate([xq[r0:r0 + nr] for r0, nr in blocks], axis=0).astype(BF16)
            rhs = xk.astype(BF16)
            for j in range(hps):
                hl = slice(j * HEAD_DIM, (j + 1) * HEAD_DIM)
                prod = _dot_nt(lhs[:, hl], rhs[:, hl])
                off = 0
                for r0, nr in blocks:
                    for p in range(0, nr, V7X_SUBLANES):
                        piece = prod[off + p:off + p + V7X_SUBLANES]
                        idx = (r0 + p) // V7X_SUBLANES
                        old = 0.0 if a[j][idx] is None else a[j][idx]
                        a[j][idx] = jnp.where(masks[(kind, n, r0 + p)], piece, old)
                    off += nr
        qe = (q * jnp.exp2(b)).astype(BF16)
        b_last = b[CHUNK - 1:CHUNK, :]
        kd = (k * jnp.exp2(b_last - b)).astype(BF16)
        a = [jnp.concatenate(aj, axis=0).astype(BF16) for aj in a]
        return a, qe, kd, jnp.exp2(b_last)

    def stage_output(g, c, ng, st, a, qe, kd, decay):
        rows = slice(c * CHUNK, (c + 1) * CHUNK)
        u_ref = u_bufs[g % 2]
        vb = u_ref[rows, 2 * gw:3 * gw].astype(BF16)
        gate = u_ref[rows, 3 * gw:4 * gw]
        for j in range(hps):
            hl = slice(j * HEAD_DIM, (j + 1) * HEAD_DIM)
            o = _dot(a[j], vb[:, hl]) + _dot(qe[:, hl], st[j].T.astype(BF16))
            st[j] = st[j] * decay[:, hl] + _dot_tn(vb[:, hl], kd[:, hl])
            ms = jnp.mean(o * o, axis=-1, keepdims=True)
            on = o * lax.rsqrt(ms + EPS) * ng[:, hl]
            gt = gate[:, hl]
            ob_ref[rows, g * gw + j * HEAD_DIM:g * gw + (j + 1) * HEAD_DIM] = (
                on * (gt * _sigmoid(gt))).astype(BF16)

    items = [(g, c) for g in range(n_groups) for c in range(n_chunks)]
    pieces = {step: [] for step in range(len(items) + 2)}
    for g in range(1, n_groups):
        for sec in range(4):
            pieces[(g - 1) * n_chunks + sec * n_chunks // 4].append(
                functools.partial(project_piece, g, sec))
    tail = (n_groups - 1) * n_chunks
    for i, (pair, half) in enumerate(
            (pair, half) for pair in range(n_groups // 2 - 1) for half in range(2)):
        pieces[tail + i * (n_chunks + 2) // (n_groups - 2)].append(
            functools.partial(out_piece, pair, half))

    for sec in range(4):
        project_piece(0, sec)
    consts, states, gates_out, intra_out = {}, {}, {}, {}
    for step in range(len(items) + 2):
        if step < len(items):
            g, c = items[step]
            if c == 0:
                consts[g] = group_consts(g)
            gates_out[(g, c)] = stage_gates(u_bufs[g % 2], c, consts[g][0], consts[g][1])
        if step >= 2:
            g, c = items[step - 2]
            if c == 0:
                states[g] = [st_ref[g * hps + j] for j in range(hps)]
            stage_output(g, c, consts[g][2], states[g], *intra_out.pop((g, c)))
            if c == n_chunks - 1:
                for j in range(hps):
                    st_ref[g * hps + j] = states[g][j]
        for piece in pieces[step]:
            piece()
        if 1 <= step <= len(items):
            intra_out[items[step - 1]] = stage_intra(*gates_out.pop(items[step - 1]))

    k_done = (n_groups // 2 - 1) * pair_k
    y = yacc_ref[...] + _dot(ob_ref[:, k_done:], wout_ref[k_done:, :])
    o_ref[0] = _rms_norm(x + y, fg_ref[...])


def _hgrn_layer(x, g, w_in, lb_params, norm_g, w_out, final_g, *, tm, layer, levels):
    bsz, seq, d_model = x.shape
    d_inner = w_out.shape[0]
    n_heads = d_inner // HEAD_DIM
    gw = HEADS_PER_STEP * HEAD_DIM
    n_groups = d_inner // gw
    depth = lb_params.shape[0]
    const2 = lambda b, t: (0, 0)
    const3 = lambda b, t: (0, 0, 0)
    const4 = lambda b, t: (0, 0, 0, 0)
    single = pl.Buffered(1)
    kern = functools.partial(_hgrn_kernel, tm=tm, d_inner=d_inner, layer=layer,
                             levels=levels)
    return pl.pallas_call(
        kern,
        grid=(bsz, seq // tm),
        in_specs=[
            pl.BlockSpec((1, tm, d_model), lambda b, t: (b, t, 0)),
            pl.BlockSpec((1, d_model), const2, pipeline_mode=single),
            pl.BlockSpec(w_in.shape, const2, pipeline_mode=single),
            pl.BlockSpec((depth, n_groups, 1, gw), const4, pipeline_mode=single),
            pl.BlockSpec((n_groups, 1, gw), const3, pipeline_mode=single),
            pl.BlockSpec(w_out.shape, const2, pipeline_mode=single),
            pl.BlockSpec((1, d_model), const2, pipeline_mode=single),
        ],
        out_specs=pl.BlockSpec((1, tm, d_model), lambda b, t: (b, t, 0)),
        out_shape=jax.ShapeDtypeStruct(x.shape, x.dtype),
        scratch_shapes=[
            pltpu.VMEM((tm, d_model), BF16),
            pltpu.VMEM((tm, 4 * gw), F32),
            pltpu.VMEM((tm, 4 * gw), F32),
            pltpu.VMEM((n_heads, HEAD_DIM, HEAD_DIM), F32),
            pltpu.VMEM((tm, d_inner), BF16),
            pltpu.VMEM((tm, d_model), F32),
        ],
        compiler_params=pltpu.CompilerParams(
            dimension_semantics=("arbitrary", "arbitrary"),
            vmem_limit_bytes=56 * 1024 * 1024,
        ),
        name=f"hgrn_layer_{len(levels)}_levels",
    )(x, g, w_in, lb_params.reshape(depth, n_groups, 1, gw),
      norm_g.reshape(n_groups, 1, gw), w_out, final_g)


def kernel(x, norm_g, pool_w_in, pool_w_grp, pool_scale, pool_w_out, hgrn_w_in,
           hgrn_lower_bounds, hgrn_norm_g, hgrn_w_out, final_g):
    depth = norm_g.shape[0]
    assert depth == 2 and pool_w_in.shape[0] == 1 and hgrn_w_in.shape[0] == 1
    h = _pool_layer(
        x, norm_g[0:1], pool_w_in[0].astype(BF16), pool_w_grp[0].astype(BF16),
        pool_scale[0:1], pool_w_out[0].astype(BF16), tm=POOL_TM)
    layer = 1
    p = jax.nn.softmax(hgrn_lower_bounds.astype(F32), axis=0)
    lb = (jnp.cumsum(p, axis=0) - p[0:1])[layer]
    fast_ok = jnp.all(DIAG_POS_STEPS * jnp.abs(jnp.log2(lb)) <= FAST_MAX_LOG2)
    operands = (h, norm_g[1:2], hgrn_w_in[0].astype(BF16), hgrn_lower_bounds,
                hgrn_norm_g[0], hgrn_w_out[0].astype(BF16), final_g.reshape(1, -1))
    return lax.cond(
        fast_ok,
        functools.partial(_hgrn_layer, tm=HGRN_TM, layer=layer, levels=LEVELS_FAST),
        functools.partial(_hgrn_layer, tm=HGRN_TM, layer=layer, levels=LEVELS_ROBUST),
        *operands)
```
